```python
import math
import jax, jax.numpy as jnp
from jax import lax
import numpy as np

D_MODEL = 2048
BATCH = 2
SEQ = 4096
DEPTH = 4
DEC_BATCH = 32
DEC_SEQ = 4
PAST_LEN = 16384
PAGE_SIZE = 128

ATTN_WIDTH = D_MODEL // 2
SSM_WIDTH = D_MODEL - ATTN_WIDTH
HEAD_DIM = 64
N_HEADS = ATTN_WIDTH // HEAD_DIM
N_KV_HEADS = max(1, N_HEADS // 8)
GQA_GROUP = N_HEADS // N_KV_HEADS
KV_WIDTH = N_KV_HEADS * HEAD_DIM
WINDOW = 128
SSM_GROUP = 16
N_SSM_GROUPS = SSM_WIDTH // SSM_GROUP
SSM_STATE = 64
D_IN = ATTN_WIDTH + 2 * KV_WIDTH + SSM_WIDTH
D_FF = ((8 * D_MODEL + 3 * 256 - 1) // (3 * 256)) * 256
EPS = 1e-5
DT_MIN = 1e-3
DT_MAX = 1e-1

kernel_name = 'hymba_swa_sink_s5_decoder_step'


def rms_norm(x, g):
    xf = x.astype(jnp.float32)
    y = xf * lax.rsqrt(jnp.mean(xf * xf, axis=-1, keepdims=True) + EPS)
    return (y * g.astype(jnp.float32)).astype(x.dtype)


def band_mask(tq, tk):
    i = jnp.arange(tq)[:, None]
    j = jnp.arange(tk)[None, :]
    return (j > i) & (j <= i + WINDOW)


def sink_attend(q, k, v, mask, sink):
    s = jnp.einsum('...qkgd,...jkd->...kgqj', q, k).astype(jnp.float32) * (HEAD_DIM ** -0.5)
    s = jnp.where(mask, s, -jnp.inf)
    sk = sink.astype(jnp.float32)[:, :, None, None]
    m = jnp.maximum(jnp.max(s, axis=-1, keepdims=True), sk)
    p = jnp.exp(s - m)
    den = jnp.sum(p, axis=-1, keepdims=True) + jnp.exp(sk - m)
    p = (p / den).astype(v.dtype)
    return jnp.einsum('...kgqj,...jkd->...qkgd', p, v)


def prompt_window_attention(q, k, v, sink):
    b, s = q.shape[0], q.shape[1]
    nb = s // WINDOW
    qb = q.reshape(b, nb, WINDOW, N_KV_HEADS, GQA_GROUP, HEAD_DIM)
    kb = k.reshape(b, nb, WINDOW, N_KV_HEADS, HEAD_DIM)
    vb = v.reshape(b, nb, WINDOW, N_KV_HEADS, HEAD_DIM)
    pad = ((0, 0), (1, 0), (0, 0), (0, 0), (0, 0))
    kk = jnp.concatenate([jnp.pad(kb[:, :-1], pad), kb], axis=2)
    vv = jnp.concatenate([jnp.pad(vb[:, :-1], pad), vb], axis=2)
    first_ok = (jnp.arange(nb) > 0)[:, None, None] | (jnp.arange(2 * WINDOW) >= WINDOW)[None, None, :]
    mask = (band_mask(WINDOW, 2 * WINDOW)[None] & first_ok)[:, None, None]
    out = sink_attend(qb, kk, vv, mask, sink).reshape(b, s, ATTN_WIDTH)
    return out, k[:, -WINDOW:], v[:, -WINDOW:]


def sample_window_attention(q, k, v, sink, kc, vc):
    b, t = q.shape[0], q.shape[1]
    kk = jnp.concatenate([kc.astype(k.dtype), k], axis=1)
    vv = jnp.concatenate([vc.astype(v.dtype), v], axis=1)
    mask = band_mask(t, WINDOW + t)
    out = sink_attend(q, kk, vv, mask, sink).reshape(b, t, ATTN_WIDTH)
    return out, kk[:, -WINDOW:], vv[:, -WINDOW:]


def complex_affine_combine(e1, e2):
    a1r, a1i, b1r, b1i = e1
    a2r, a2i, b2r, b2i = e2
    ar = a1r * a2r - a1i * a2i
    ai = a1r * a2i + a1i * a2r
    br = a2r * b1r - a2i * b1i + b2r
    bi = a2r * b1i + a2i * b1r + b2i
    return (ar, ai, br, bi)


def s5_branch(u, h0r, h0i, a_re, a_im, log_dt, b_re, b_im, c_re, c_im, d, w_glu):
    f32 = jnp.float32
    bsz, t = u.shape[0], u.shape[1]
    uf = u.astype(f32)
    ug = uf.reshape(bsz, t, N_SSM_GROUPS, SSM_GROUP)
    dt = jnp.exp(log_dt.astype(f32))[:, None]
    lr = a_re.astype(f32)
    li = a_im.astype(f32)
    mag = jnp.exp(lr * dt)
    ang = li * dt
    abr = mag * jnp.cos(ang)
    abi = mag * jnp.sin(ang)
    nr = abr - 1.0
    den = lr * lr + li * li
    cr = (nr * lr + abi * li) / den
    ci = (abi * lr - nr * li) / den
    br_ = b_re.astype(f32)
    bi_ = b_im.astype(f32)
    bbr = cr[..., None] * br_ - ci[..., None] * bi_
    bbi = cr[..., None] * bi_ + ci[..., None] * br_
    bur = jnp.einsum('blgc,gpc->blgp', ug, bbr)
    bui = jnp.einsum('blgc,gpc->blgp', ug, bbi)
    ar = jnp.broadcast_to(abr, bur.shape)
    ai = jnp.broadcast_to(abi, bur.shape)
    acr, aci, bcr, bci = lax.associative_scan(complex_affine_combine, (ar, ai, bur, bui), axis=1)
    h0r_ = h0r.astype(f32)[:, None]
    h0i_ = h0i.astype(f32)[:, None]
    hr = acr * h0r_ - aci * h0i_ + bcr
    hi = acr * h0i_ + aci * h0r_ + bci
    y = (jnp.einsum('blgp,gcp->blgc', hr, c_re.astype(f32))
         - jnp.einsum('blgp,gcp->blgc', hi, c_im.astype(f32)))
    y = y.reshape(bsz, t, SSM_WIDTH) + d.astype(f32) * uf
    z = jax.nn.gelu(y, approximate=False).astype(u.dtype)
    out = z * jax.nn.sigmoid(z @ w_glu)
    return out, hr[:, -1], hi[:, -1]


def hybrid_layer(x, l, p, kc, vc, h0r, h0i):
    b, t = x.shape[0], x.shape[1]
    h = rms_norm(x, p['norm_mix'][l])
    z = h @ p['w_in'][l]
    q = z[..., :ATTN_WIDTH].reshape(b, t, N_KV_HEADS, GQA_GROUP, HEAD_DIM)
    k = z[..., ATTN_WIDTH:ATTN_WIDTH + KV_WIDTH].reshape(b, t, N_KV_HEADS, HEAD_DIM)
    v = z[..., ATTN_WIDTH + KV_WIDTH:ATTN_WIDTH + 2 * KV_WIDTH].reshape(b, t, N_KV_HEADS, HEAD_DIM)
    u = z[..., ATTN_WIDTH + 2 * KV_WIDTH:]
    sink = p['attn_sink'][l].reshape(N_KV_HEADS, GQA_GROUP)
    if kc is None:
        a, kn, vn = prompt_window_attention(q, k, v, sink)
    else:
        a, kn, vn = sample_window_attention(q, k, v, sink, kc, vc)
    s, hr, hi = s5_branch(u, h0r, h0i, p['ssm_a_re'][l], p['ssm_a_im'][l], p['ssm_log_dt'][l],
                          p['ssm_b_re'][l], p['ssm_b_im'][l], p['ssm_c_re'][l], p['ssm_c_im'][l],
                          p['ssm_d'][l], p['w_glu'][l])
    mixed = jnp.concatenate([rms_norm(a, p['norm_attn_out'][l]),
                             rms_norm(s, p['norm_ssm_out'][l])], axis=-1) @ p['w_out'][l]
    x = x + mixed
    hf = rms_norm(x, p['norm_ffn'][l])
    x = x + (jax.nn.silu(hf @ p['w_gate'][l]) * (hf @ p['w_up'][l])) @ p['w_down'][l]
    return x, kn, vn, hr, hi


def run_trunk(x, p, cache_k, cache_v, h0_re, h0_im):
    ks, vs, hrs, his = [], [], [], []
    for l in range(DEPTH):
        kc = None if cache_k is None else cache_k[l]
        vc = None if cache_v is None else cache_v[l]
        x, kn, vn, hr, hi = hybrid_layer(x, l, p, kc, vc, h0_re[l], h0_im[l])
        ks.append(kn)
        vs.append(vn)
        hrs.append(hr)
        his.append(hi)
    y = rms_norm(x, p['norm_final'])
    return y, jnp.stack(ks), jnp.stack(vs), jnp.stack(hrs), jnp.stack(his)


def setup_inputs(seed: int = 0) -> dict:
    key = jax.random.key(seed)
    ks = jax.random.split(key, 32)
    f32 = jnp.float32

    def nrm(k, shape, scale):
        return jax.random.normal(k, shape, f32) * scale

    G, P = N_SSM_GROUPS, SSM_STATE
    return {
        'x_prompt': nrm(ks[0], (BATCH, SEQ, D_MODEL), 1.0),
        'x_sample': nrm(ks[1], (DEC_BATCH, DEC_SEQ, D_MODEL), 1.0),
        'cache_k': nrm(ks[2], (DEPTH, DEC_BATCH, WINDOW, N_KV_HEADS, HEAD_DIM), 1.0),
        'cache_v': nrm(ks[3], (DEPTH, DEC_BATCH, WINDOW, N_KV_HEADS, HEAD_DIM), 1.0),
        'state_ssm_re': nrm(ks[4], (DEPTH, DEC_BATCH, G, P), 0.1),
        'state_ssm_im': nrm(ks[5], (DEPTH, DEC_BATCH, G, P), 0.1),
        'norm_mix': 1.0 + nrm(ks[6], (DEPTH, D_MODEL), 0.02),
        'w_in': nrm(ks[7], (DEPTH, D_MODEL, D_IN), D_MODEL ** -0.5),
        'attn_sink': nrm(ks[8], (DEPTH, N_HEADS), 0.5),
        'ssm_a_re': -0.5 * jnp.exp(nrm(ks[9], (DEPTH, G, P), 0.05)),
        'ssm_a_im': math.pi * jnp.arange(P, dtype=f32) + nrm(ks[10], (DEPTH, G, P), 0.01),
        'ssm_log_dt': jax.random.uniform(ks[11], (DEPTH, G), f32, math.log(DT_MIN), math.log(DT_MAX)),
        'ssm_b_re': nrm(ks[12], (DEPTH, G, P, SSM_GROUP), (2 * SSM_GROUP) ** -0.5),
        'ssm_b_im': nrm(ks[13], (DEPTH, G, P, SSM_GROUP), (2 * SSM_GROUP) ** -0.5),
        'ssm_c_re': nrm(ks[14], (DEPTH, G, SSM_GROUP, P), (2 * P) ** -0.5),
        'ssm_c_im': nrm(ks[15], (DEPTH, G, SSM_GROUP, P), (2 * P) ** -0.5),
        'ssm_d': nrm(ks[16], (DEPTH, SSM_WIDTH), 1.0),
        'w_glu': nrm(ks[17], (DEPTH, SSM_WIDTH, SSM_WIDTH), SSM_WIDTH ** -0.5),
        'norm_attn_out': 1.0 + nrm(ks[18], (DEPTH, ATTN_WIDTH), 0.02),
        'norm_ssm_out': 1.0 + nrm(ks[19], (DEPTH, SSM_WIDTH), 0.02),
        'w_out': nrm(ks[20], (DEPTH, ATTN_WIDTH + SSM_WIDTH, D_MODEL), (ATTN_WIDTH + SSM_WIDTH) ** -0.5),
        'norm_ffn': 1.0 + nrm(ks[21], (DEPTH, D_MODEL), 0.02),
        'w_gate': nrm(ks[22], (DEPTH, D_MODEL, D_FF), D_MODEL ** -0.5),
        'w_up': nrm(ks[23], (DEPTH, D_MODEL, D_FF), D_MODEL ** -0.5),
        'w_down': nrm(ks[24], (DEPTH, D_FF, D_MODEL), D_FF ** -0.5),
        'norm_final': 1.0 + nrm(ks[25], (D_MODEL,), 0.02),
    }


def reference(x_prompt, x_sample, cache_k, cache_v, state_ssm_re, state_ssm_im,
              norm_mix, w_in, attn_sink, ssm_a_re, ssm_a_im, ssm_log_dt,
              ssm_b_re, ssm_b_im, ssm_c_re, ssm_c_im, ssm_d, w_glu,
              norm_attn_out, norm_ssm_out, w_out, norm_ffn, w_gate, w_up, w_down,
              norm_final):
    p = dict(norm_mix=norm_mix, w_in=w_in, attn_sink=attn_sink,
             ssm_a_re=ssm_a_re, ssm_a_im=ssm_a_im, ssm_log_dt=ssm_log_dt,
             ssm_b_re=ssm_b_re, ssm_b_im=ssm_b_im, ssm_c_re=ssm_c_re, ssm_c_im=ssm_c_im,
             ssm_d=ssm_d, w_glu=w_glu, norm_attn_out=norm_attn_out, norm_ssm_out=norm_ssm_out,
             w_out=w_out, norm_ffn=norm_ffn, w_gate=w_gate, w_up=w_up, w_down=w_down,
             norm_final=norm_final)
    zeros_state = jnp.zeros((DEPTH, x_prompt.shape[0], N_SSM_GROUPS, SSM_STATE), jnp.float32)
    y_prompt, k_p, v_p, hr_p, hi_p = run_trunk(x_prompt, p, None, None, zeros_state, zeros_state)
    y_sample, k_s, v_s, hr_s, hi_s = run_trunk(x_sample, p, cache_k, cache_v, state_ssm_re, state_ssm_im)
    return (y_prompt, y_sample, k_p, v_p, hr_p, hi_p, k_s, v_s, hr_s, hi_s)
```

```python
import functools
import math

import jax
import jax.numpy as jnp
from jax import lax
from jax.experimental import pallas as pl
from jax.experimental.pallas import tpu as pltpu

F32 = jnp.float32
BF16 = jnp.bfloat16

D_MODEL = 2048
BATCH = 2
SEQ = 4096
DEPTH = 4
DEC_BATCH = 32
DEC_SEQ = 4
ATTN_WIDTH = 1024
SSM_WIDTH = 1024
HEAD_DIM = 64
N_HEADS = 16
N_KV_HEADS = 2
GQA_GROUP = 8
KV_WIDTH = 128
WINDOW = 128
SSM_GROUP = 16
N_SSM_GROUPS = 64
SSM_STATE = 64
D_IN = ATTN_WIDTH + 2 * KV_WIDTH + SSM_WIDTH
D_FF = 5632
EPS = 1e-5

T_PROMPT = BATCH * SEQ
T_SAMPLE = DEC_BATCH * DEC_SEQ
T_ALL = T_PROMPT + T_SAMPLE
ROW_TILE = 640
FF_TILE = 512
CHUNK_P = 16
CHUNK_S = DEC_SEQ
N_CHUNKS_SEQ = SEQ // CHUNK_P
N_CHUNKS = T_PROMPT // CHUNK_P
MASK_NEG = -1e30
VMEM_LIMIT = 56 * 1024 * 1024

_TRANS_B = (((1,), (1,)), ((), ()))
_HI = lax.Precision.HIGHEST


def _cparams(*sem):
    return pltpu.CompilerParams(dimension_semantics=sem, vmem_limit_bytes=VMEM_LIMIT)


def _rms(x, g):
    ms = jnp.mean(x * x, axis=-1, keepdims=True)
    return x * lax.rsqrt(ms + EPS) * g


def _in_proj_kernel(x_ref, g_ref, w_ref, q_ref, kv_ref, u_ref):
    h = _rms(x_ref[...], g_ref[...]).astype(BF16)
    z = jnp.dot(h, w_ref[...], preferred_element_type=F32)
    q_ref[...] = (z[:, :ATTN_WIDTH] * (HEAD_DIM ** -0.5)).astype(BF16)
    kv_ref[...] = z[:, ATTN_WIDTH:ATTN_WIDTH + 2 * KV_WIDTH]
    u_ref[...] = z[:, ATTN_WIDTH + 2 * KV_WIDTH:]


def _in_proj(x_all, g, w_in_b, l):
    return pl.pallas_call(
        _in_proj_kernel,
        grid=(T_ALL // ROW_TILE,),
        in_specs=[
            pl.BlockSpec((ROW_TILE, D_MODEL), lambda i: (i, 0)),
            pl.BlockSpec((None, 1, D_MODEL), lambda i: (l, 0, 0)),
            pl.BlockSpec((None, D_MODEL, D_IN), lambda i: (l, 0, 0), pipeline_mode=pl.Buffered(1)),
        ],
        out_specs=[
            pl.BlockSpec((ROW_TILE, ATTN_WIDTH), lambda i: (i, 0)),
            pl.BlockSpec((ROW_TILE, 2 * KV_WIDTH), lambda i: (i, 0)),
            pl.BlockSpec((ROW_TILE, SSM_WIDTH), lambda i: (i, 0)),
        ],
        out_shape=[
            jax.ShapeDtypeStruct((T_ALL, ATTN_WIDTH), BF16),
            jax.ShapeDtypeStruct((T_ALL, 2 * KV_WIDTH), F32),
            jax.ShapeDtypeStruct((T_ALL, SSM_WIDTH), F32),
        ],
        compiler_params=_cparams("parallel"),
        name="in_proj",
    )(x_all, g, w_in_b)


def _softmax_pv(s_parts, v_parts, sinks, rows):
    p_parts = [[] for _ in s_parts]
    dens = []
    for g in range(GQA_GROUP):
        sl = slice(g * rows, (g + 1) * rows)
        sk = sinks[g]
        m = sk
        for s in s_parts:
            m = jnp.maximum(jnp.max(s[sl], axis=-1, keepdims=True), m)
        den = jnp.exp(sk - m)
        for j, s in enumerate(s_parts):
            p = jnp.exp(s[sl] - m)
            den = den + jnp.sum(p, axis=-1, keepdims=True)
            p_parts[j].append(p.astype(BF16))
        dens.append(den)
    o = None
    for j, v in enumerate(v_parts):
        pj = jnp.concatenate(p_parts[j], axis=0)
        oj = jnp.dot(pj, v, preferred_element_type=F32)
        o = oj if o is None else o + oj
    return [o[g * rows:(g + 1) * rows] / dens[g] for g in range(GQA_GROUP)]


def _attn_prompt_kernel(sink_ref, q_ref, kvc_ref, kvp_ref, g_ref, o_ref):
    n = pl.program_id(0) % (SEQ // WINDOW)
    row = lax.broadcasted_iota(jnp.int32, (WINDOW, 2 * WINDOW), 0)
    col = lax.broadcasted_iota(jnp.int32, (WINDOW, 2 * WINDOW), 1)
    ok = (col > row) & (col <= row + WINDOW) & ((col >= WINDOW) | (n > 0))
    bias = jnp.where(ok, 0.0, MASK_NEG).astype(F32)
    bias = jnp.concatenate([bias] * GQA_GROUP, axis=0)
    q = q_ref[...]
    kvc = kvc_ref[...]
    kvp = kvp_ref[...]
    outs = []
    for hk in range(N_KV_HEADS):
        ks = slice(hk * HEAD_DIM, (hk + 1) * HEAD_DIM)
        vs = slice(KV_WIDTH + hk * HEAD_DIM, KV_WIDTH + (hk + 1) * HEAD_DIM)
        k = jnp.concatenate([kvp[:, ks], kvc[:, ks]], axis=0).astype(BF16)
        v = jnp.concatenate([kvp[:, vs], kvc[:, vs]], axis=0).astype(BF16)
        qs = jnp.concatenate(
            [q[:, (hk * GQA_GROUP + g) * HEAD_DIM:(hk * GQA_GROUP + g + 1) * HEAD_DIM] for g in range(GQA_GROUP)],
            axis=0)
        s = lax.dot_general(qs, k, _TRANS_B, preferred_element_type=F32) + bias
        sinks = [sink_ref[hk * GQA_GROUP + g] for g in range(GQA_GROUP)]
        outs += _softmax_pv([s], [v], sinks, WINDOW)
    a = jnp.concatenate(outs, axis=1)
    o_ref[...] = _rms(a, g_ref[...]).astype(BF16)


def _attn_prompt(sink, q_all, kv_all, g, l):
    nb = SEQ // WINDOW
    kv_col = 0
    return pl.pallas_call(
        _attn_prompt_kernel,
        grid=(T_PROMPT // WINDOW,),
        in_specs=[
            pl.BlockSpec(memory_space=pltpu.SMEM),
            pl.BlockSpec((WINDOW, ATTN_WIDTH), lambda i: (i, 0)),
            pl.BlockSpec((WINDOW, 2 * KV_WIDTH), lambda i: (i, kv_col)),
            pl.BlockSpec((WINDOW, 2 * KV_WIDTH), lambda i: (jnp.where(i % nb == 0, i, i - 1), kv_col)),
            pl.BlockSpec((None, 1, ATTN_WIDTH), lambda i: (l, 0, 0)),
        ],
        out_specs=pl.BlockSpec((WINDOW, ATTN_WIDTH), lambda i: (i, 0)),
        out_shape=jax.ShapeDtypeStruct((T_ALL, ATTN_WIDTH), BF16),
        compiler_params=_cparams("parallel"),
        name="attn_prompt",
    )(sink, q_all, kv_all, kv_all, g)


_SEQ_PER_STEP = 4
_ROWS_S = _SEQ_PER_STEP * DEC_SEQ


def _attn_sample_kernel(sink_ref, q_ref, kv_ref, kc_ref, vc_ref, g_ref, a_in_ref, o_ref):
    del a_in_ref
    rows = GQA_GROUP * _ROWS_S
    r = lax.broadcasted_iota(jnp.int32, (rows, _SEQ_PER_STEP * WINDOW), 0) % _ROWS_S
    c = lax.broadcasted_iota(jnp.int32, (rows, _SEQ_PER_STEP * WINDOW), 1)
    ok_c = (c // WINDOW == r // DEC_SEQ) & (c % WINDOW > r % DEC_SEQ)
    bias_c = jnp.where(ok_c, 0.0, MASK_NEG).astype(F32)
    r2 = lax.broadcasted_iota(jnp.int32, (rows, _ROWS_S), 0) % _ROWS_S
    c2 = lax.broadcasted_iota(jnp.int32, (rows, _ROWS_S), 1)
    ok_n = (c2 // DEC_SEQ == r2 // DEC_SEQ) & (c2 % DEC_SEQ <= r2 % DEC_SEQ)
    bias_n = jnp.where(ok_n, 0.0, MASK_NEG).astype(F32)
    q = q_ref[...]
    kv = kv_ref[...]
    outs = []
    for hk in range(N_KV_HEADS):
        ks = slice(hk * HEAD_DIM, (hk + 1) * HEAD_DIM)
        vs = slice(KV_WIDTH + hk * HEAD_DIM, KV_WIDTH + (hk + 1) * HEAD_DIM)
        k_c = jnp.concatenate([kc_ref[b][:, ks] for b in range(_SEQ_PER_STEP)], axis=0).astype(BF16)
        v_c = jnp.concatenate([vc_ref[b][:, ks] for b in range(_SEQ_PER_STEP)], axis=0).astype(BF16)
        k_n = kv[:, ks].astype(BF16)
        v_n = kv[:, vs].astype(BF16)
        qs = jnp.concatenate(
            [q[:, (hk * GQA_GROUP + g) * HEAD_DIM:(hk * GQA_GROUP + g + 1) * HEAD_DIM] for g in range(GQA_GROUP)],
            axis=0)
        s_c = lax.dot_general(qs, k_c, _TRANS_B, preferred_element_type=F32) + bias_c
        s_n = lax.dot_general(qs, k_n, _TRANS_B, preferred_element_type=F32) + bias_n
        sinks = [sink_ref[hk * GQA_GROUP + g] for g in range(GQA_GROUP)]
        outs += _softmax_pv([s_c, s_n], [v_c, v_n], sinks, _ROWS_S)
    a = jnp.concatenate(outs, axis=1)
    o_ref[...] = _rms(a, g_ref[...]).astype(BF16)


def _attn_sample(sink, q_all, kv_all, kc, vc, g, a_prompt, l):
    base = T_PROMPT // _ROWS_S
    return pl.pallas_call(
        _attn_sample_kernel,
        grid=(DEC_BATCH // _SEQ_PER_STEP,),
        in_specs=[
            pl.BlockSpec(memory_space=pltpu.SMEM),
            pl.BlockSpec((_ROWS_S, ATTN_WIDTH), lambda i: (base + i, 0)),
            pl.BlockSpec((_ROWS_S, 2 * KV_WIDTH), lambda i: (base + i, 0)),
            pl.BlockSpec((None, _SEQ_PER_STEP, WINDOW, KV_WIDTH), lambda i: (l, i, 0, 0)),
            pl.BlockSpec((None, _SEQ_PER_STEP, WINDOW, KV_WIDTH), lambda i: (l, i, 0, 0)),
            pl.BlockSpec((None, 1, ATTN_WIDTH), lambda i: (l, 0, 0)),
            pl.BlockSpec(memory_space=pl.ANY),
        ],
        out_specs=pl.BlockSpec((_ROWS_S, ATTN_WIDTH), lambda i: (base + i, 0)),
        out_shape=jax.ShapeDtypeStruct((T_ALL, ATTN_WIDTH), BF16),
        input_output_aliases={6: 0},
        compiler_params=_cparams("parallel"),
        name="attn_sample",
    )(sink, q_all, kv_all, kc, vc, g, a_prompt)


def _ssm_prep_kernel(L, lam_row_ref, lam_col_ref, logdt_ref, bre_ref, bim_ref, cre_ref, cim_ref,
                     toe_ref, bst_ref, cst_ref, al_ref):
    P, C = SSM_STATE, SSM_GROUP
    dt = jnp.exp(logdt_ref[...])
    lam_row = lam_row_ref[...]
    lr, li = lam_row[:, :P], lam_row[:, P:]
    n_tau = ((L + 1 + 7) // 8) * 8
    tau = lax.broadcasted_iota(jnp.int32, (n_tau, P), 0).astype(F32)
    mag = jnp.exp(tau * (lr * dt))
    ang = tau * (li * dt)
    pw_r, pw_i = mag * jnp.cos(ang), mag * jnp.sin(ang)
    cre, cim = cre_ref[...], cim_ref[...]
    gt_r = [cre * pw_r[t:t + 1] - cim * pw_i[t:t + 1] for t in range(L + 1)]
    gt_i = [cre * pw_i[t:t + 1] + cim * pw_r[t:t + 1] for t in range(L + 1)]
    cst_ref[...] = jnp.concatenate(
        [jnp.concatenate(gt_r[1:], axis=0), -jnp.concatenate(gt_i[1:], axis=0)], axis=1)
    al_ref[...] = jnp.concatenate([pw_r[L:L + 1], pw_i[L:L + 1]], axis=1)

    lam_col = lam_col_ref[...]
    lr_c, li_c = lam_col[:, 0:1], lam_col[:, 1:2]
    mag_c = jnp.exp(lr_c * dt)
    abr, abi = mag_c * jnp.cos(li_c * dt), mag_c * jnp.sin(li_c * dt)
    nr = abr - 1.0
    den = lr_c * lr_c + li_c * li_c
    cr = (nr * lr_c + abi * li_c) / den
    ci = (abi * lr_c - nr * li_c) / den
    bre, bim = bre_ref[...], bim_ref[...]
    bbr = cr * bre - ci * bim
    bbi = cr * bim + ci * bre

    k_t = (jnp.dot(jnp.concatenate(gt_r[:L], axis=0), bbr, preferred_element_type=F32, precision=_HI)
           - jnp.dot(jnp.concatenate(gt_i[:L], axis=0), bbi, preferred_element_type=F32, precision=_HI))
    tau_l = lax.broadcasted_iota(jnp.int32, (P, 128), 1).astype(F32)
    mag_l = jnp.exp(tau_l * (lr_c * dt))
    ang_l = tau_l * (li_c * dt)
    pc_r, pc_i = mag_l * jnp.cos(ang_l), mag_l * jnp.sin(ang_l)
    for s in range(L):
        col = k_t if s == 0 else jnp.concatenate([jnp.zeros((s * C, C), F32), k_t[:(L - s) * C]], axis=0)
        toe_ref[:, s * C:(s + 1) * C] = col
        e = L - 1 - s
        wr, wi = pc_r[:, e:e + 1], pc_i[:, e:e + 1]
        bst_ref[0:P, s * C:(s + 1) * C] = wr * bbr - wi * bbi
        bst_ref[P:2 * P, s * C:(s + 1) * C] = wr * bbi + wi * bbr


def _ssm_prep(L, lam_row, lam_col, logdt, bre, bim, cre, cim):
    G, P, C = N_SSM_GROUPS, SSM_STATE, SSM_GROUP
    W = L * C
    g3 = lambda *blk: pl.BlockSpec((None,) + blk, lambda g: (g, 0, 0))
    return pl.pallas_call(
        functools.partial(_ssm_prep_kernel, L),
        grid=(G,),
        in_specs=[g3(1, 2 * P), g3(P, 2), g3(1, 1), g3(P, C), g3(P, C), g3(C, P), g3(C, P)],
        out_specs=[g3(W, W), g3(2 * P, W), g3(W, 2 * P), g3(1, 2 * P)],
        out_shape=[
            jax.ShapeDtypeStruct((G, W, W), F32),
            jax.ShapeDtypeStruct((G, 2 * P, W), F32),
            jax.ShapeDtypeStruct((G, W, 2 * P), F32),
            jax.ShapeDtypeStruct((G, 1, 2 * P), F32),
        ],
        compiler_params=_cparams("parallel"),
        name=f"ssm_prep_{L}",
    )(lam_row, lam_col, logdt, bre, bim, cre, cim)


def _complex_scale(x, a):
    P = SSM_STATE
    a_r, a_i = a[:, :P], a[:, P:]
    a1 = jnp.concatenate([a_r, a_r], axis=1)
    a2 = jnp.concatenate([-a_i, a_i], axis=1)
    return x * a1 + pltpu.roll(x, P, axis=1) * a2


def _ssm_prompt_kernel(u_ref, toe_ref, bst_ref, cst_ref, al_ref, y_ref, sfin_ref):
    P = SSM_STATE
    u = u_ref[...]
    y_loc = lax.dot_general(u, toe_ref[...].astype(BF16), _TRANS_B, preferred_element_type=F32)
    s = lax.dot_general(u, bst_ref[...].astype(BF16), _TRANS_B, preferred_element_type=F32)
    jj = lax.broadcasted_iota(jnp.int32, (N_CHUNKS, 2 * P), 0) % N_CHUNKS_SEQ
    a = al_ref[...]
    d = 1
    while d < N_CHUNKS_SEQ:
        sh = jnp.where(jj >= d, pltpu.roll(s, d, axis=0), 0.0)
        s = s + _complex_scale(sh, a)
        a_r, a_i = a[:, :P], a[:, P:]
        a = jnp.concatenate([a_r * a_r - a_i * a_i, 2.0 * a_r * a_i], axis=1)
        d *= 2
    s_in = jnp.where(jj >= 1, pltpu.roll(s, 1, axis=0), 0.0)
    y_st = lax.dot_general(s_in.astype(BF16), cst_ref[...].astype(BF16), _TRANS_B, preferred_element_type=F32)
    y_ref[...] = y_loc + y_st
    for b in range(BATCH):
        sfin_ref[b:b + 1, :] = s[(b + 1) * N_CHUNKS_SEQ - 1:(b + 1) * N_CHUNKS_SEQ, :]


def _ssm_prompt(u_g, toe, bst, cst, al):
    G, P = N_SSM_GROUPS, SSM_STATE
    W = CHUNK_P * SSM_GROUP
    g3 = lambda *blk: pl.BlockSpec((None,) + blk, lambda g: (g, 0, 0))
    return pl.pallas_call(
        _ssm_prompt_kernel,
        grid=(G,),
        in_specs=[g3(N_CHUNKS, W), g3(W, W), g3(2 * P, W), g3(W, 2 * P), g3(1, 2 * P)],
        out_specs=[g3(N_CHUNKS, W), g3(BATCH, 2 * P)],
        out_shape=[
            jax.ShapeDtypeStruct((G, N_CHUNKS, W), F32),
            jax.ShapeDtypeStruct((G, BATCH, 2 * P), F32),
        ],
        compiler_params=_cparams("parallel"),
        name="ssm_prompt",
    )(u_g, toe, bst, cst, al)


def _ssm_sample_kernel(u_ref, h0_ref, toe_ref, bst_ref, cst_ref, al_ref, y_ref, snew_ref):
    u = u_ref[...]
    h0 = h0_ref[...]
    y_loc = lax.dot_general(u, toe_ref[...], _TRANS_B, preferred_element_type=F32, precision=_HI)
    e = lax.dot_general(u, bst_ref[...], _TRANS_B, preferred_element_type=F32, precision=_HI)
    y_st = lax.dot_general(h0, cst_ref[...], _TRANS_B, preferred_element_type=F32, precision=_HI)
    y_ref[...] = y_loc + y_st
    snew_ref[...] = _complex_scale(h0, al_ref[...]) + e


def _ssm_sample(u_g, h0_g, toe, bst, cst, al):
    G, P = N_SSM_GROUPS, SSM_STATE
    W = CHUNK_S * SSM_GROUP
    g3 = lambda *blk: pl.BlockSpec((None,) + blk, lambda g: (g, 0, 0))
    return pl.pallas_call(
        _ssm_sample_kernel,
        grid=(G,),
        in_specs=[g3(DEC_BATCH, W), g3(DEC_BATCH, 2 * P), g3(W, W), g3(2 * P, W), g3(W, 2 * P), g3(1, 2 * P)],
        out_specs=[g3(DEC_BATCH, W), g3(DEC_BATCH, 2 * P)],
        out_shape=[
            jax.ShapeDtypeStruct((G, DEC_BATCH, W), F32),
            jax.ShapeDtypeStruct((G, DEC_BATCH, 2 * P), F32),
        ],
        compiler_params=_cparams("parallel"),
        name="ssm_sample",
    )(u_g, h0_g, toe, bst, cst, al)


def _glu_kernel(y_ref, u_ref, d_ref, w_ref, g_ref, o_ref):
    y = y_ref[...] + d_ref[...] * u_ref[...]
    z = 0.5 * y * (1.0 + lax.erf(y * (1.0 / math.sqrt(2.0))))
    gate = jnp.dot(z.astype(BF16), w_ref[...], preferred_element_type=F32)
    out = z * jax.nn.sigmoid(gate)
    o_ref[...] = _rms(out, g_ref[...]).astype(BF16)


def _glu(y_all, u_all, d, w_glu_b, g, l):
    row = lambda i: (i, 0)
    par = lambda i: (l, 0, 0)
    return pl.pallas_call(
        _glu_kernel,
        grid=(T_ALL // ROW_TILE,),
        in_specs=[
            pl.BlockSpec((ROW_TILE, SSM_WIDTH), row),
            pl.BlockSpec((ROW_TILE, SSM_WIDTH), row),
            pl.BlockSpec((None, 1, SSM_WIDTH), par),
            pl.BlockSpec((None, SSM_WIDTH, SSM_WIDTH), par, pipeline_mode=pl.Buffered(1)),
            pl.BlockSpec((None, 1, SSM_WIDTH), par),
        ],
        out_specs=pl.BlockSpec((ROW_TILE, SSM_WIDTH), row),
        out_shape=jax.ShapeDtypeStruct((T_ALL, SSM_WIDTH), BF16),
        compiler_params=_cparams("parallel"),
        name="glu",
    )(y_all, u_all, d, w_glu_b, g)


def _out_proj_kernel(a_ref, s_ref, w_ref, x_ref, o_ref):
    mixed = jnp.dot(jnp.concatenate([a_ref[...], s_ref[...]], axis=1), w_ref[...], preferred_element_type=F32)
    o_ref[...] = x_ref[...] + mixed


def _out_proj(a_n, s_n, w_out_b, x_all, l):
    row = lambda i: (i, 0)
    return pl.pallas_call(
        _out_proj_kernel,
        grid=(T_ALL // ROW_TILE,),
        in_specs=[
            pl.BlockSpec((ROW_TILE, ATTN_WIDTH), row),
            pl.BlockSpec((ROW_TILE, SSM_WIDTH), row),
            pl.BlockSpec((None, D_MODEL, D_MODEL), lambda i: (l, 0, 0), pipeline_mode=pl.Buffered(1)),
            pl.BlockSpec((ROW_TILE, D_MODEL), row),
        ],
        out_specs=pl.BlockSpec((ROW_TILE, D_MODEL), row),
        out_shape=jax.ShapeDtypeStruct((T_ALL, D_MODEL), F32),
        compiler_params=_cparams("parallel"),
        name="out_proj",
    )(a_n, s_n, w_out_b, x_all)


def _ffn_kernel(x_ref, g_ref, wg_ref, wu_ref, wd_ref, o_ref, h_ref):
    @pl.when(pl.program_id(1) == 0)
    def _():
        x = x_ref[...]
        h_ref[...] = _rms(x, g_ref[...]).astype(BF16)
        o_ref[...] = x

    h = h_ref[...]
    gate = jnp.dot(h, wg_ref[...], preferred_element_type=F32)
    up = jnp.dot(h, wu_ref[...], preferred_element_type=F32)
    act = (gate * jax.nn.sigmoid(gate) * up).astype(BF16)
    o_ref[...] += jnp.dot(act, wd_ref[...], preferred_element_type=F32)


def _ffn(x_all, g, w_gate_b, w_up_b, w_down_b, l):
    return pl.pallas_call(
        _ffn_kernel,
        grid=(T_ALL // ROW_TILE, D_FF // FF_TILE),
        in_specs=[
            pl.BlockSpec((ROW_TILE, D_MODEL), lambda i, f: (i, 0)),
            pl.BlockSpec((None, 1, D_MODEL), lambda i, f: (l, 0, 0)),
            pl.BlockSpec((None, D_MODEL, FF_TILE), lambda i, f: (l, 0, f)),
            pl.BlockSpec((None, D_MODEL, FF_TILE), lambda i, f: (l, 0, f)),
            pl.BlockSpec((None, FF_TILE, D_MODEL), lambda i, f: (l, f, 0)),
        ],
        out_specs=pl.BlockSpec((ROW_TILE, D_MODEL), lambda i, f: (i, 0)),
        out_shape=jax.ShapeDtypeStruct((T_ALL, D_MODEL), F32),
        scratch_shapes=[pltpu.VMEM((ROW_TILE, D_MODEL), BF16)],
        compiler_params=_cparams("parallel", "arbitrary"),
        name="ffn",
    )(x_all, g, w_gate_b, w_up_b, w_down_b)


def _norm_kernel(x_ref, g_ref, o_ref):
    o_ref[...] = _rms(x_ref[...], g_ref[...])


def _final_norm(x_all, g, rows, tile, first_block):
    return pl.pallas_call(
        _norm_kernel,
        grid=(rows // tile,),
        in_specs=[
            pl.BlockSpec((tile, D_MODEL), lambda i: (first_block + i, 0)),
            pl.BlockSpec((1, D_MODEL), lambda i: (0, 0)),
        ],
        out_specs=pl.BlockSpec((tile, D_MODEL), lambda i: (i, 0)),
        out_shape=jax.ShapeDtypeStruct((rows, D_MODEL), F32),
        compiler_params=_cparams("parallel"),
        name="final_norm",
    )(x_all, g)


def _to_groups(u, n_chunks, chunk):
    return (u.reshape(n_chunks, chunk, N_SSM_GROUPS, SSM_GROUP).transpose(2, 0, 1, 3)
            .reshape(N_SSM_GROUPS, n_chunks, chunk * SSM_GROUP))


def _from_groups(y, n_chunks, chunk):
    return (y.reshape(N_SSM_GROUPS, n_chunks, chunk, SSM_GROUP).transpose(1, 2, 0, 3)
            .reshape(n_chunks * chunk, SSM_WIDTH))


def kernel(x_prompt, x_sample, cache_k, cache_v, state_ssm_re, state_ssm_im, norm_mix, w_in, attn_sink, ssm_a_re,
           ssm_a_im, ssm_log_dt, ssm_b_re, ssm_b_im, ssm_c_re, ssm_c_im, ssm_d, w_glu, norm_attn_out, norm_ssm_out,
           w_out, norm_ffn, w_gate, w_up, w_down, norm_final):
    G, P = N_SSM_GROUPS, SSM_STATE
    x_all = jnp.concatenate([x_prompt.reshape(T_PROMPT, D_MODEL), x_sample.reshape(T_SAMPLE, D_MODEL)], axis=0)
    w_in_b, w_glu_b, w_out_b = w_in.astype(BF16), w_glu.astype(BF16), w_out.astype(BF16)
    w_gate_b, w_up_b, w_down_b = w_gate.astype(BF16), w_up.astype(BF16), w_down.astype(BF16)
    row3 = lambda p: p.reshape(DEPTH, 1, -1)
    g_mix, g_attn, g_ssm, g_ffn, d3 = row3(norm_mix), row3(norm_attn_out), row3(norm_ssm_out), row3(norm_ffn), row3(ssm_d)
    cache_k2 = cache_k.reshape(DEPTH, DEC_BATCH, WINDOW, KV_WIDTH)
    cache_v2 = cache_v.reshape(DEPTH, DEC_BATCH, WINDOW, KV_WIDTH)

    k_p, v_p, hr_p, hi_p, k_s, v_s, hr_s, hi_s = [], [], [], [], [], [], [], []
    for l in range(DEPTH):
        q_all, kv_all, u_all = _in_proj(x_all, g_mix, w_in_b, l)

        a_n = _attn_prompt(attn_sink[l], q_all, kv_all, g_attn, l)
        a_n = _attn_sample(attn_sink[l], q_all, kv_all, cache_k2, cache_v2, g_attn, a_n, l)

        lam_row = jnp.concatenate([ssm_a_re[l], ssm_a_im[l]], axis=1).reshape(G, 1, 2 * P)
        lam_col = jnp.stack([ssm_a_re[l], ssm_a_im[l]], axis=-1)
        logdt = ssm_log_dt[l].reshape(G, 1, 1)
        prep_args = (lam_row, lam_col, logdt, ssm_b_re[l], ssm_b_im[l], ssm_c_re[l], ssm_c_im[l])
        toe_p, bst_p, cst_p, al_p = _ssm_prep(CHUNK_P, *prep_args)
        toe_s, bst_s, cst_s, al_s = _ssm_prep(CHUNK_S, *prep_args)

        u_gp = _to_groups(u_all[:T_PROMPT].astype(BF16), N_CHUNKS, CHUNK_P)
        y_gp, sfin = _ssm_prompt(u_gp, toe_p, bst_p, cst_p, al_p)
        u_gs = _to_groups(u_all[T_PROMPT:], DEC_BATCH, CHUNK_S)
        h0_g = jnp.concatenate([state_ssm_re[l], state_ssm_im[l]], axis=-1).transpose(1, 0, 2)
        y_gs, snew = _ssm_sample(u_gs, h0_g, toe_s, bst_s, cst_s, al_s)
        y_all = jnp.concatenate([_from_groups(y_gp, N_CHUNKS, CHUNK_P), _from_groups(y_gs, DEC_BATCH, CHUNK_S)], axis=0)
        s_n = _glu(y_all, u_all, d3, w_glu_b, g_ssm, l)

        x_all = _out_proj(a_n, s_n, w_out_b, x_all, l)
        x_all = _ffn(x_all, g_ffn, w_gate_b, w_up_b, w_down_b, l)

        kv_p = kv_all[:T_PROMPT].reshape(BATCH, SEQ, 2, N_KV_HEADS, HEAD_DIM)[:, -WINDOW:]
        k_p.append(kv_p[:, :, 0])
        v_p.append(kv_p[:, :, 1])
        kv_s = kv_all[T_PROMPT:].reshape(DEC_BATCH, DEC_SEQ, 2, N_KV_HEADS, HEAD_DIM)
        k_s.append(jnp.concatenate([cache_k[l][:, DEC_SEQ:], kv_s[:, :, 0]], axis=1))
        v_s.append(jnp.concatenate([cache_v[l][:, DEC_SEQ:], kv_s[:, :, 1]], axis=1))
        sfin = sfin.transpose(1, 0, 2)
        hr_p.append(sfin[..., :P])
        hi_p.append(sfin[..., P:])
        snew = snew.transpose(1, 0, 2)
        hr_s.append(snew[..., :P])
        hi_s.append(snew[..., P:])

    g_fin = norm_final.reshape(1, D_MODEL)
    y_prompt = _final_norm(x_all, g_fin, T_PROMPT, 1024, 0).reshape(BATCH, SEQ, D_MODEL)
    y_sample = _final_norm(x_all, g_fin, T_SAMPLE, T_SAMPLE, T_PROMPT // T_SAMPLE).reshape(DEC_BATCH, DEC_SEQ, D_MODEL)
    st = jnp.stack
    return (y_prompt, y_sample, st(k_p), st(v_p), st(hr_p), st(hi_p), st(k_s), st(v_s), st(hr_s), st(hi_s))
```

```python
import functools
import math

import jax
import jax.numpy as jnp
from jax import lax
from jax.experimental import pallas as pl
from jax.experimental.pallas import tpu as pltpu

F32 = jnp.float32
BF16 = jnp.bfloat16

D_MODEL = 2048
BATCH = 2
SEQ = 4096
DEPTH = 4
DEC_BATCH = 32
DEC_SEQ = 4
ATTN_WIDTH = 1024
SSM_WIDTH = 1024
HEAD_DIM = 64
N_HEADS = 16
N_KV_HEADS = 2
GQA_GROUP = 8
KV_WIDTH = 128
WINDOW = 128
SSM_GROUP = 16
N_SSM_GROUPS = 64
SSM_STATE = 64
D_IN = ATTN_WIDTH + 2 * KV_WIDTH + SSM_WIDTH
D_FF = 5632
EPS = 1e-5

T_PROMPT = BATCH * SEQ
T_SAMPLE = DEC_BATCH * DEC_SEQ
T_ALL = T_PROMPT + T_SAMPLE
ROW_TILE = 640
FF_TILE = 512
CHUNK_P = 16
CHUNK_S = DEC_SEQ
SLAB = 128
GROUPS_PER_SLAB = SLAB // SSM_GROUP
N_SLABS = SSM_WIDTH // SLAB
SLAB_STATE = GROUPS_PER_SLAB * SSM_STATE
MASK_NEG = -1e30
VMEM_LIMIT = 56 * 1024 * 1024

_TRANS_B = (((1,), (1,)), ((), ()))
_HI = lax.Precision.HIGHEST


def _cparams(*sem):
    return pltpu.CompilerParams(dimension_semantics=sem, vmem_limit_bytes=VMEM_LIMIT)


def _rms(x, g):
    ms = jnp.mean(x * x, axis=-1, keepdims=True)
    return x * lax.rsqrt(ms + EPS) * g


def _in_proj_kernel(x_ref, g_ref, w_ref, q_ref, kv_ref, u_ref):
    h = _rms(x_ref[...], g_ref[...]).astype(BF16)
    z = jnp.dot(h, w_ref[...], preferred_element_type=F32)
    q_ref[...] = (z[:, :ATTN_WIDTH] * (HEAD_DIM ** -0.5)).astype(BF16)
    kv_ref[...] = z[:, ATTN_WIDTH:ATTN_WIDTH + 2 * KV_WIDTH]
    u_ref[...] = z[:, ATTN_WIDTH + 2 * KV_WIDTH:]


def _in_proj(x_all, g, w_in_b, l):
    return pl.pallas_call(
        _in_proj_kernel,
        grid=(T_ALL // ROW_TILE,),
        in_specs=[
            pl.BlockSpec((ROW_TILE, D_MODEL), lambda i: (i, 0)),
            pl.BlockSpec((None, 1, D_MODEL), lambda i: (l, 0, 0)),
            pl.BlockSpec((None, D_MODEL, D_IN), lambda i: (l, 0, 0), pipeline_mode=pl.Buffered(1)),
        ],
        out_specs=[
            pl.BlockSpec((ROW_TILE, ATTN_WIDTH), lambda i: (i, 0)),
            pl.BlockSpec((ROW_TILE, 2 * KV_WIDTH), lambda i: (i, 0)),
            pl.BlockSpec((ROW_TILE, SSM_WIDTH), lambda i: (i, 0)),
        ],
        out_shape=[
            jax.ShapeDtypeStruct((T_ALL, ATTN_WIDTH), BF16),
            jax.ShapeDtypeStruct((T_ALL, 2 * KV_WIDTH), F32),
            jax.ShapeDtypeStruct((T_ALL, SSM_WIDTH), F32),
        ],
        compiler_params=_cparams("parallel"),
        name="in_proj",
    )(x_all, g, w_in_b)


def _softmax_pv(s_parts, v_parts, sinks, rows):
    p_parts = [[] for _ in s_parts]
    dens = []
    for g in range(GQA_GROUP):
        sl = slice(g * rows, (g + 1) * rows)
        sk = sinks[g]
        m = sk
        for s in s_parts:
            m = jnp.maximum(jnp.max(s[sl], axis=-1, keepdims=True), m)
        den = jnp.exp(sk - m)
        for j, s in enumerate(s_parts):
            p = jnp.exp(s[sl] - m)
            den = den + jnp.sum(p, axis=-1, keepdims=True)
            p_parts[j].append(p.astype(BF16))
        dens.append(den)
    o = None
    for j, v in enumerate(v_parts):
        pj = jnp.concatenate(p_parts[j], axis=0)
        oj = jnp.dot(pj, v, preferred_element_type=F32)
        o = oj if o is None else o + oj
    return [o[g * rows:(g + 1) * rows] / dens[g] for g in range(GQA_GROUP)]


def _attn_prompt_kernel(sink_ref, q_ref, kvc_ref, kvp_ref, g_ref, o_ref):
    n = pl.program_id(0) % (SEQ // WINDOW)
    row = lax.broadcasted_iota(jnp.int32, (WINDOW, 2 * WINDOW), 0)
    col = lax.broadcasted_iota(jnp.int32, (WINDOW, 2 * WINDOW), 1)
    ok = (col > row) & (col <= row + WINDOW) & ((col >= WINDOW) | (n > 0))
    bias = jnp.where(ok, 0.0, MASK_NEG).astype(F32)
    bias = jnp.concatenate([bias] * GQA_GROUP, axis=0)
    q = q_ref[...]
    kvc = kvc_ref[...]
    kvp = kvp_ref[...]
    outs = []
    for hk in range(N_KV_HEADS):
        ks = slice(hk * HEAD_DIM, (hk + 1) * HEAD_DIM)
        vs = slice(KV_WIDTH + hk * HEAD_DIM, KV_WIDTH + (hk + 1) * HEAD_DIM)
        k = jnp.concatenate([kvp[:, ks], kvc[:, ks]], axis=0).astype(BF16)
        v = jnp.concatenate([kvp[:, vs], kvc[:, vs]], axis=0).astype(BF16)
        qs = jnp.concatenate(
            [q[:, (hk * GQA_GROUP + g) * HEAD_DIM:(hk * GQA_GROUP + g + 1) * HEAD_DIM] for g in range(GQA_GROUP)],
            axis=0)
        s = lax.dot_general(qs, k, _TRANS_B, preferred_element_type=F32) + bias
        sinks = [sink_ref[hk * GQA_GROUP + g] for g in range(GQA_GROUP)]
        outs += _softmax_pv([s], [v], sinks, WINDOW)
    a = jnp.concatenate(outs, axis=1)
    o_ref[...] = _rms(a, g_ref[...]).astype(BF16)


def _attn_prompt(sink, q_all, kv_all, g, l):
    nb = SEQ // WINDOW
    return pl.pallas_call(
        _attn_prompt_kernel,
        grid=(T_PROMPT // WINDOW,),
        in_specs=[
            pl.BlockSpec(memory_space=pltpu.SMEM),
            pl.BlockSpec((WINDOW, ATTN_WIDTH), lambda i: (i, 0)),
            pl.BlockSpec((WINDOW, 2 * KV_WIDTH), lambda i: (i, 0)),
            pl.BlockSpec((WINDOW, 2 * KV_WIDTH), lambda i: (jnp.where(i % nb == 0, i, i - 1), 0)),
            pl.BlockSpec((None, 1, ATTN_WIDTH), lambda i: (l, 0, 0)),
        ],
        out_specs=pl.BlockSpec((WINDOW, ATTN_WIDTH), lambda i: (i, 0)),
        out_shape=jax.ShapeDtypeStruct((T_ALL, ATTN_WIDTH), BF16),
        compiler_params=_cparams("parallel"),
        name="attn_prompt",
    )(sink, q_all, kv_all, kv_all, g)


_SEQ_PER_STEP = 4
_ROWS_S = _SEQ_PER_STEP * DEC_SEQ


def _attn_sample_kernel(sink_ref, q_ref, kv_ref, kc_ref, vc_ref, g_ref, a_in_ref, o_ref):
    del a_in_ref
    rows = GQA_GROUP * _ROWS_S
    r = lax.broadcasted_iota(jnp.int32, (rows, _SEQ_PER_STEP * WINDOW), 0) % _ROWS_S
    c = lax.broadcasted_iota(jnp.int32, (rows, _SEQ_PER_STEP * WINDOW), 1)
    ok_c = (c // WINDOW == r // DEC_SEQ) & (c % WINDOW > r % DEC_SEQ)
    bias_c = jnp.where(ok_c, 0.0, MASK_NEG).astype(F32)
    r2 = lax.broadcasted_iota(jnp.int32, (rows, _ROWS_S), 0) % _ROWS_S
    c2 = lax.broadcasted_iota(jnp.int32, (rows, _ROWS_S), 1)
    ok_n = (c2 // DEC_SEQ == r2 // DEC_SEQ) & (c2 % DEC_SEQ <= r2 % DEC_SEQ)
    bias_n = jnp.where(ok_n, 0.0, MASK_NEG).astype(F32)
    q = q_ref[...]
    kv = kv_ref[...]
    outs = []
    for hk in range(N_KV_HEADS):
        ks = slice(hk * HEAD_DIM, (hk + 1) * HEAD_DIM)
        vs = slice(KV_WIDTH + hk * HEAD_DIM, KV_WIDTH + (hk + 1) * HEAD_DIM)
        k_c = jnp.concatenate([kc_ref[b][:, ks] for b in range(_SEQ_PER_STEP)], axis=0).astype(BF16)
        v_c = jnp.concatenate([vc_ref[b][:, ks] for b in range(_SEQ_PER_STEP)], axis=0).astype(BF16)
        k_n = kv[:, ks].astype(BF16)
        v_n = kv[:, vs].astype(BF16)
        qs = jnp.concatenate(
            [q[:, (hk * GQA_GROUP + g) * HEAD_DIM:(hk * GQA_GROUP + g + 1) * HEAD_DIM] for g in range(GQA_GROUP)],
            axis=0)
        s_c = lax.dot_general(qs, k_c, _TRANS_B, preferred_element_type=F32) + bias_c
        s_n = lax.dot_general(qs, k_n, _TRANS_B, preferred_element_type=F32) + bias_n
        sinks = [sink_ref[hk * GQA_GROUP + g] for g in range(GQA_GROUP)]
        outs += _softmax_pv([s_c, s_n], [v_c, v_n], sinks, _ROWS_S)
    a = jnp.concatenate(outs, axis=1)
    o_ref[...] = _rms(a, g_ref[...]).astype(BF16)


def _attn_sample(sink, q_all, kv_all, kc, vc, g, a_prompt, l):
    base = T_PROMPT // _ROWS_S
    return pl.pallas_call(
        _attn_sample_kernel,
        grid=(DEC_BATCH // _SEQ_PER_STEP,),
        in_specs=[
            pl.BlockSpec(memory_space=pltpu.SMEM),
            pl.BlockSpec((_ROWS_S, ATTN_WIDTH), lambda i: (base + i, 0)),
            pl.BlockSpec((_ROWS_S, 2 * KV_WIDTH), lambda i: (base + i, 0)),
            pl.BlockSpec((None, _SEQ_PER_STEP, WINDOW, KV_WIDTH), lambda i: (l, i, 0, 0)),
            pl.BlockSpec((None, _SEQ_PER_STEP, WINDOW, KV_WIDTH), lambda i: (l, i, 0, 0)),
            pl.BlockSpec((None, 1, ATTN_WIDTH), lambda i: (l, 0, 0)),
            pl.BlockSpec(memory_space=pl.ANY),
        ],
        out_specs=pl.BlockSpec((_ROWS_S, ATTN_WIDTH), lambda i: (base + i, 0)),
        out_shape=jax.ShapeDtypeStruct((T_ALL, ATTN_WIDTH), BF16),
        input_output_aliases={6: 0},
        compiler_params=_cparams("parallel"),
        name="attn_sample",
    )(sink, q_all, kv_all, kc, vc, g, a_prompt)


def _expand_groups(m):
    m2 = m.reshape(SLAB, SSM_STATE)
    t = jnp.concatenate([m2] * GROUPS_PER_SLAB, axis=1)
    r = lax.broadcasted_iota(jnp.int32, (SLAB, SLAB_STATE), 0) // SSM_GROUP
    c = lax.broadcasted_iota(jnp.int32, (SLAB, SLAB_STATE), 1) // SSM_STATE
    return jnp.where(r == c, t, 0.0)


def _fold_groups(x):
    t = [x[:, i * SLAB:(i + 1) * SLAB] for i in range(2 * SLAB_STATE // SLAB)]
    return jnp.concatenate([t[0] + t[1] + t[2] + t[3], t[4] + t[5] + t[6] + t[7]], axis=1)


def _ssm_prep_kernel(L, lam_ref, logdt_ref, bre_ref, bim_ref, cre_ref, cim_ref, tt_ref, bst_ref, cst_ref, al_ref):
    H = SLAB_STATE
    dt = jnp.exp(logdt_ref[...])
    lam = lam_ref[...]
    lr, li = lam[0:1], lam[1:2]
    n_tau = ((L + 1 + 7) // 8) * 8
    tau = lax.broadcasted_iota(jnp.int32, (n_tau, H), 0).astype(F32)
    mag = jnp.exp(tau * (lr * dt))
    ang = tau * (li * dt)
    pw_r, pw_i = mag * jnp.cos(ang), mag * jnp.sin(ang)
    abr, abi = pw_r[1:2], pw_i[1:2]
    nr = abr - 1.0
    den = lr * lr + li * li
    cr = (nr * lr + abi * li) / den
    ci = (abi * lr - nr * li) / den
    xb_r, xb_i = _expand_groups(bre_ref[...]), _expand_groups(bim_ref[...])
    xc_r, xc_i = _expand_groups(cre_ref[...]), _expand_groups(cim_ref[...])
    ccm = _fold_groups(jnp.concatenate([xc_r, -xc_i], axis=1))
    same_group = (lax.broadcasted_iota(jnp.int32, (SLAB, SLAB), 0) // SSM_GROUP
                  == lax.broadcasted_iota(jnp.int32, (SLAB, SLAB), 1) // SSM_GROUP)

    k_lag = [None] * L
    for s in range(L):
        e = L - 1 - s
        w_r = cr * pw_r[e:e + 1] - ci * pw_i[e:e + 1]
        w_i = cr * pw_i[e:e + 1] + ci * pw_r[e:e + 1]
        slab = jnp.concatenate([xb_r * w_r - xb_i * w_i, xb_r * w_i + xb_i * w_r], axis=1)
        bst_ref[s * SLAB:(s + 1) * SLAB, :] = slab.astype(BF16)
        k_all = lax.dot_general(_fold_groups(slab), ccm, _TRANS_B, preferred_element_type=F32, precision=_HI)
        k_lag[e] = jnp.where(same_group, k_all, 0.0)
    for s in range(L):
        for t_lo in range(2):
            lag = L - 2 + t_lo - s
            blk = k_lag[lag] if lag >= 0 else jnp.zeros((SLAB, SLAB), F32)
            tt_ref[s * SLAB:(s + 1) * SLAB, t_lo * SLAB:(t_lo + 1) * SLAB] = blk.astype(BF16)
    for t in range(L):
        p_r, p_i = pw_r[t + 1:t + 2], pw_i[t + 1:t + 2]
        g_t = jnp.concatenate([xc_r * p_r - xc_i * p_i, -(xc_r * p_i + xc_i * p_r)], axis=1)
        cst_ref[:, t * SLAB:(t + 1) * SLAB] = g_t.T.astype(BF16)
    al_ref[...] = jnp.concatenate([pw_r[L:L + 1], pw_i[L:L + 1]], axis=0)


def _ssm_prep(L, lam, logdt, bt_re, bt_im, c_re, c_im):
    H = SLAB_STATE
    q3 = lambda *blk: pl.BlockSpec((None,) + blk, lambda q: (q, 0, 0))
    grp = pl.BlockSpec((GROUPS_PER_SLAB, SSM_GROUP, SSM_STATE), lambda q: (q, 0, 0))
    return pl.pallas_call(
        functools.partial(_ssm_prep_kernel, L),
        grid=(N_SLABS,),
        in_specs=[q3(2, H), q3(1, H), grp, grp, grp, grp],
        out_specs=[q3(L * SLAB, 2 * SLAB), q3(L * SLAB, 2 * H), q3(2 * H, L * SLAB), q3(2, H)],
        out_shape=[
            jax.ShapeDtypeStruct((N_SLABS, L * SLAB, 2 * SLAB), BF16),
            jax.ShapeDtypeStruct((N_SLABS, L * SLAB, 2 * H), BF16),
            jax.ShapeDtypeStruct((N_SLABS, 2 * H, L * SLAB), BF16),
            jax.ShapeDtypeStruct((N_SLABS, 2, H), F32),
        ],
        compiler_params=_cparams("parallel"),
        name=f"ssm_prep_{L}",
    )(lam, logdt, bt_re, bt_im, c_re, c_im)


def _ssm_kernel(L, n_per_seq, with_state, *refs):
    if with_state:
        u_ref, h0r_ref, h0i_ref, tt_ref, bst_ref, cst_ref, al_ref, _, y_ref, sr_ref, si_ref = refs
    else:
        u_ref, tt_ref, bst_ref, cst_ref, al_ref, y_ref, sr_ref, si_ref = refs
    H = SLAB_STATE
    nj = u_ref.shape[0] // L
    lhs = jnp.concatenate([u_ref[pl.ds(s, nj, stride=L), :].astype(BF16) for s in range(L)], axis=1)
    e = jnp.dot(lhs, bst_ref[...], preferred_element_type=F32)
    s_r, s_i = e[:, :H], e[:, H:]
    al = al_ref[...]
    a_r, a_i = al[0:1], al[1:2]
    if with_state:
        in_r, in_i = h0r_ref[...], h0i_ref[...]
        sr_ref[...] = s_r + in_r * a_r - in_i * a_i
        si_ref[...] = s_i + in_r * a_i + in_i * a_r
    else:
        jj = lax.broadcasted_iota(jnp.int32, (nj, H), 0) % n_per_seq
        d = 1
        while d < n_per_seq:
            keep = jj >= d
            sh_r = jnp.where(keep, pltpu.roll(s_r, d, axis=0), 0.0)
            sh_i = jnp.where(keep, pltpu.roll(s_i, d, axis=0), 0.0)
            s_r, s_i = s_r + sh_r * a_r - sh_i * a_i, s_i + sh_r * a_i + sh_i * a_r
            a_r, a_i = a_r * a_r - a_i * a_i, 2.0 * a_r * a_i
            d *= 2
        in_r = jnp.where(jj >= 1, pltpu.roll(s_r, 1, axis=0), 0.0)
        in_i = jnp.where(jj >= 1, pltpu.roll(s_i, 1, axis=0), 0.0)
        for b in range(nj // n_per_seq):
            last = (b + 1) * n_per_seq - 1
            sr_ref[b:b + 1, :] = s_r[last:last + 1]
            si_ref[b:b + 1, :] = s_i[last:last + 1]
    y_st = jnp.dot(jnp.concatenate([in_r, in_i], axis=1).astype(BF16), cst_ref[...], preferred_element_type=F32)
    for tp in range(L // 2):
        k = 2 * SLAB * (tp + 1)
        y = jnp.dot(lhs[:, :k], tt_ref[2 * SLAB * (L // 2 - 1 - tp):, :], preferred_element_type=F32)
        y = y + y_st[:, 2 * SLAB * tp:2 * SLAB * (tp + 1)]
        y_ref[pl.ds(2 * tp, nj, stride=L), :] = y[:, :SLAB]
        y_ref[pl.ds(2 * tp + 1, nj, stride=L), :] = y[:, SLAB:]


def _ssm_operator_specs(L):
    H = SLAB_STATE
    q3 = lambda *blk: pl.BlockSpec((None,) + blk, lambda q: (q, 0, 0))
    return [q3(L * SLAB, 2 * SLAB), q3(L * SLAB, 2 * H), q3(2 * H, L * SLAB), q3(2, H)]


def _ssm_prompt(u_all, ops):
    H = SLAB_STATE
    return pl.pallas_call(
        functools.partial(_ssm_kernel, CHUNK_P, SEQ // CHUNK_P, False),
        grid=(N_SLABS,),
        in_specs=[pl.BlockSpec((T_PROMPT, SLAB), lambda q: (0, q))] + _ssm_operator_specs(CHUNK_P),
        out_specs=[
            pl.BlockSpec((T_PROMPT, SLAB), lambda q: (0, q)),
            pl.BlockSpec((BATCH, H), lambda q: (0, q)),
            pl.BlockSpec((BATCH, H), lambda q: (0, q)),
        ],
        out_shape=[
            jax.ShapeDtypeStruct((T_ALL, SSM_WIDTH), F32),
            jax.ShapeDtypeStruct((BATCH, N_SLABS * H), F32),
            jax.ShapeDtypeStruct((BATCH, N_SLABS * H), F32),
        ],
        compiler_params=_cparams("parallel"),
        name="ssm_prompt",
    )(u_all, *ops)


def _ssm_sample(u_all, h0_r, h0_i, ops, y_prompt):
    H = SLAB_STATE
    row_blk = T_PROMPT // T_SAMPLE
    return pl.pallas_call(
        functools.partial(_ssm_kernel, CHUNK_S, 1, True),
        grid=(N_SLABS,),
        in_specs=[
            pl.BlockSpec((T_SAMPLE, SLAB), lambda q: (row_blk, q)),
            pl.BlockSpec((DEC_BATCH, H), lambda q: (0, q)),
            pl.BlockSpec((DEC_BATCH, H), lambda q: (0, q)),
        ] + _ssm_operator_specs(CHUNK_S) + [pl.BlockSpec(memory_space=pl.ANY)],
        out_specs=[
            pl.BlockSpec((T_SAMPLE, SLAB), lambda q: (row_blk, q)),
            pl.BlockSpec((DEC_BATCH, H), lambda q: (0, q)),
            pl.BlockSpec((DEC_BATCH, H), lambda q: (0, q)),
        ],
        out_shape=[
            jax.ShapeDtypeStruct((T_ALL, SSM_WIDTH), F32),
            jax.ShapeDtypeStruct((DEC_BATCH, N_SLABS * H), F32),
            jax.ShapeDtypeStruct((DEC_BATCH, N_SLABS * H), F32),
        ],
        input_output_aliases={7: 0},
        compiler_params=_cparams("parallel"),
        name="ssm_sample",
    )(u_all, h0_r, h0_i, *ops, y_prompt)


def _glu_kernel(y_ref, u_ref, d_ref, w_ref, g_ref, o_ref):
    y = y_ref[...] + d_ref[...] * u_ref[...]
    z = 0.5 * y * (1.0 + lax.erf(y * (1.0 / math.sqrt(2.0))))
    gate = jnp.dot(z.astype(BF16), w_ref[...], preferred_element_type=F32)
    out = z * jax.nn.sigmoid(gate)
    o_ref[...] = _rms(out, g_ref[...]).astype(BF16)


def _glu(y_all, u_all, d, w_glu_b, g, l):
    row = lambda i: (i, 0)
    par = lambda i: (l, 0, 0)
    return pl.pallas_call(
        _glu_kernel,
        grid=(T_ALL // ROW_TILE,),
        in_specs=[
            pl.BlockSpec((ROW_TILE, SSM_WIDTH), row),
            pl.BlockSpec((ROW_TILE, SSM_WIDTH), row),
            pl.BlockSpec((None, 1, SSM_WIDTH), par),
            pl.BlockSpec((None, SSM_WIDTH, SSM_WIDTH), par, pipeline_mode=pl.Buffered(1)),
            pl.BlockSpec((None, 1, SSM_WIDTH), par),
        ],
        out_specs=pl.BlockSpec((ROW_TILE, SSM_WIDTH), row),
        out_shape=jax.ShapeDtypeStruct((T_ALL, SSM_WIDTH), BF16),
        compiler_params=_cparams("parallel"),
        name="glu",
    )(y_all, u_all, d, w_glu_b, g)


def _out_proj_kernel(a_ref, s_ref, w_ref, x_ref, o_ref):
    mixed = jnp.dot(jnp.concatenate([a_ref[...], s_ref[...]], axis=1), w_ref[...], preferred_element_type=F32)
    o_ref[...] = x_ref[...] + mixed


def _out_proj(a_n, s_n, w_out_b, x_all, l):
    row = lambda i: (i, 0)
    return pl.pallas_call(
        _out_proj_kernel,
        grid=(T_ALL // ROW_TILE,),
        in_specs=[
            pl.BlockSpec((ROW_TILE, ATTN_WIDTH), row),
            pl.BlockSpec((ROW_TILE, SSM_WIDTH), row),
            pl.BlockSpec((None, D_MODEL, D_MODEL), lambda i: (l, 0, 0), pipeline_mode=pl.Buffered(1)),
            pl.BlockSpec((ROW_TILE, D_MODEL), row),
        ],
        out_specs=pl.BlockSpec((ROW_TILE, D_MODEL), row),
        out_shape=jax.ShapeDtypeStruct((T_ALL, D_MODEL), F32),
        compiler_params=_cparams("parallel"),
        name="out_proj",
    )(a_n, s_n, w_out_b, x_all)


def _ffn_kernel(x_ref, g_ref, wg_ref, wu_ref, wd_ref, o_ref, h_ref):
    @pl.when(pl.program_id(1) == 0)
    def _():
        x = x_ref[...]
        h_ref[...] = _rms(x, g_ref[...]).astype(BF16)
        o_ref[...] = x

    h = h_ref[...]
    gate = jnp.dot(h, wg_ref[...], preferred_element_type=F32)
    up = jnp.dot(h, wu_ref[...], preferred_element_type=F32)
    act = (gate * jax.nn.sigmoid(gate) * up).astype(BF16)
    o_ref[...] += jnp.dot(act, wd_ref[...], preferred_element_type=F32)


def _ffn(x_all, g, w_gate_b, w_up_b, w_down_b, l):
    return pl.pallas_call(
        _ffn_kernel,
        grid=(T_ALL // ROW_TILE, D_FF // FF_TILE),
        in_specs=[
            pl.BlockSpec((ROW_TILE, D_MODEL), lambda i, f: (i, 0)),
            pl.BlockSpec((None, 1, D_MODEL), lambda i, f: (l, 0, 0)),
            pl.BlockSpec((None, D_MODEL, FF_TILE), lambda i, f: (l, 0, f)),
            pl.BlockSpec((None, D_MODEL, FF_TILE), lambda i, f: (l, 0, f)),
            pl.BlockSpec((None, FF_TILE, D_MODEL), lambda i, f: (l, f, 0)),
        ],
        out_specs=pl.BlockSpec((ROW_TILE, D_MODEL), lambda i, f: (i, 0)),
        out_shape=jax.ShapeDtypeStruct((T_ALL, D_MODEL), F32),
        scratch_shapes=[pltpu.VMEM((ROW_TILE, D_MODEL), BF16)],
        compiler_params=_cparams("parallel", "arbitrary"),
        name="ffn",
    )(x_all, g, w_gate_b, w_up_b, w_down_b)


def _norm_kernel(x_ref, g_ref, o_ref):
    o_ref[...] = _rms(x_ref[...], g_ref[...])


def _final_norm(x_all, g, rows, tile, first_block):
    return pl.pallas_call(
        _norm_kernel,
        grid=(rows // tile,),
        in_specs=[
            pl.BlockSpec((tile, D_MODEL), lambda i: (first_block + i, 0)),
            pl.BlockSpec((1, D_MODEL), lambda i: (0, 0)),
        ],
        out_specs=pl.BlockSpec((tile, D_MODEL), lambda i: (i, 0)),
        out_shape=jax.ShapeDtypeStruct((rows, D_MODEL), F32),
        compiler_params=_cparams("parallel"),
        name="final_norm",
    )(x_all, g)


def kernel(x_prompt, x_sample, cache_k, cache_v, state_ssm_re, state_ssm_im, norm_mix, w_in, attn_sink, ssm_a_re,
           ssm_a_im, ssm_log_dt, ssm_b_re, ssm_b_im, ssm_c_re, ssm_c_im, ssm_d, w_glu, norm_attn_out, norm_ssm_out,
           w_out, norm_ffn, w_gate, w_up, w_down, norm_final):
    G, P, H = N_SSM_GROUPS, SSM_STATE, SLAB_STATE
    x_all = jnp.concatenate([x_prompt.reshape(T_PROMPT, D_MODEL), x_sample.reshape(T_SAMPLE, D_MODEL)], axis=0)
    w_in_b, w_glu_b, w_out_b = w_in.astype(BF16), w_glu.astype(BF16), w_out.astype(BF16)
    w_gate_b, w_up_b, w_down_b = w_gate.astype(BF16), w_up.astype(BF16), w_down.astype(BF16)
    row3 = lambda p: p.reshape(DEPTH, 1, -1)
    g_mix, g_attn, g_ssm, g_ffn, d3 = row3(norm_mix), row3(norm_attn_out), row3(norm_ssm_out), row3(norm_ffn), row3(ssm_d)
    cache_k2 = cache_k.reshape(DEPTH, DEC_BATCH, WINDOW, KV_WIDTH)
    cache_v2 = cache_v.reshape(DEPTH, DEC_BATCH, WINDOW, KV_WIDTH)

    k_p, v_p, hr_p, hi_p, k_s, v_s, hr_s, hi_s = [], [], [], [], [], [], [], []
    for l in range(DEPTH):
        q_all, kv_all, u_all = _in_proj(x_all, g_mix, w_in_b, l)

        a_n = _attn_prompt(attn_sink[l], q_all, kv_all, g_attn, l)
        a_n = _attn_sample(attn_sink[l], q_all, kv_all, cache_k2, cache_v2, g_attn, a_n, l)

        lam = jnp.stack([ssm_a_re[l].reshape(N_SLABS, H), ssm_a_im[l].reshape(N_SLABS, H)], axis=1)
        logdt = jnp.repeat(ssm_log_dt[l], P).reshape(N_SLABS, 1, H)
        prep_args = (lam, logdt, ssm_b_re[l].transpose(0, 2, 1), ssm_b_im[l].transpose(0, 2, 1),
                     ssm_c_re[l], ssm_c_im[l])
        ops_p = _ssm_prep(CHUNK_P, *prep_args)
        ops_s = _ssm_prep(CHUNK_S, *prep_args)
        y_all, sfin_r, sfin_i = _ssm_prompt(u_all, ops_p)
        y_all, snew_r, snew_i = _ssm_sample(u_all, state_ssm_re[l].reshape(DEC_BATCH, G * P),
                                            state_ssm_im[l].reshape(DEC_BATCH, G * P), ops_s, y_all)
        s_n = _glu(y_all, u_all, d3, w_glu_b, g_ssm, l)

        x_all = _out_proj(a_n, s_n, w_out_b, x_all, l)
        x_all = _ffn(x_all, g_ffn, w_gate_b, w_up_b, w_down_b, l)

        kv_p = kv_all[:T_PROMPT].reshape(BATCH, SEQ, 2, N_KV_HEADS, HEAD_DIM)[:, -WINDOW:]
        k_p.append(kv_p[:, :, 0])
        v_p.append(kv_p[:, :, 1])
        kv_s = kv_all[T_PROMPT:].reshape(DEC_BATCH, DEC_SEQ, 2, N_KV_HEADS, HEAD_DIM)
        k_s.append(jnp.concatenate([cache_k[l][:, DEC_SEQ:], kv_s[:, :, 0]], axis=1))
        v_s.append(jnp.concatenate([cache_v[l][:, DEC_SEQ:], kv_s[:, :, 1]], axis=1))
        hr_p.append(sfin_r.reshape(BATCH, G, P))
        hi_p.append(sfin_i.reshape(BATCH, G, P))
        hr_s.append(snew_r.reshape(DEC_BATCH, G, P))
        hi_s.append(snew_i.reshape(DEC_BATCH, G, P))

    g_fin = norm_final.reshape(1, D_MODEL)
    y_prompt = _final_norm(x_all, g_fin, T_PROMPT, 1024, 0).reshape(BATCH, SEQ, D_MODEL)
    y_sample = _final_norm(x_all, g_fin, T_SAMPLE, T_SAMPLE, T_PROMPT // T_SAMPLE).reshape(DEC_BATCH, DEC_SEQ, D_MODEL)
    st = jnp.stack
    return (y_prompt, y_sample, st(k_p), st(v_p), st(hr_p), st(hi_p), st(k_s), st(v_s), st(hr_s), st(hi_s))
```

```python
import functools
import math

import jax
import jax.numpy as jnp
from jax import lax
from jax.experimental import pallas as pl
from jax.experimental.pallas import tpu as pltpu

F32 = jnp.float32
BF16 = jnp.bfloat16

D_MODEL = 2048
BATCH = 2
SEQ = 4096
DEPTH = 4
DEC_BATCH = 32
DEC_SEQ = 4
ATTN_WIDTH = 1024
SSM_WIDTH = 1024
HEAD_DIM = 64
N_HEADS = 16
N_KV_HEADS = 2
GQA_GROUP = 8
KV_WIDTH = 128
WINDOW = 128
SSM_GROUP = 16
N_SSM_GROUPS = 64
SSM_STATE = 64
D_IN = ATTN_WIDTH + 2 * KV_WIDTH + SSM_WIDTH
D_FF = 5632
EPS = 1e-5

T_PROMPT = BATCH * SEQ
T_SAMPLE = DEC_BATCH * DEC_SEQ
T_ALL = T_PROMPT + T_SAMPLE
ROW_TILE = 1040
FF_TILE = 512
CHUNK_P = 16
CHUNK_S = DEC_SEQ
SLAB = 128
GROUPS_PER_SLAB = SLAB // SSM_GROUP
N_SLABS = SSM_WIDTH // SLAB
SLAB_STATE = GROUPS_PER_SLAB * SSM_STATE
MASK_NEG = -1e30
VMEM_LIMIT = 56 * 1024 * 1024

_TRANS_B = (((1,), (1,)), ((), ()))
_HI = lax.Precision.HIGHEST


def _cparams(*sem):
    return pltpu.CompilerParams(dimension_semantics=sem, vmem_limit_bytes=VMEM_LIMIT)


def _rms(x, g):
    ms = jnp.mean(x * x, axis=-1, keepdims=True)
    return x * lax.rsqrt(ms + EPS) * g


def _in_proj_kernel(x_ref, g_ref, w_ref, q_ref, kv_ref, u_ref):
    h = _rms(x_ref[...], g_ref[...]).astype(BF16)
    z = jnp.dot(h, w_ref[...], preferred_element_type=F32)
    q_ref[...] = (z[:, :ATTN_WIDTH] * (HEAD_DIM ** -0.5)).astype(BF16)
    kv_ref[...] = z[:, ATTN_WIDTH:ATTN_WIDTH + 2 * KV_WIDTH]
    u_ref[...] = z[:, ATTN_WIDTH + 2 * KV_WIDTH:]


def _in_proj(x_all, g, w_in_b, l):
    return pl.pallas_call(
        _in_proj_kernel,
        grid=(T_ALL // ROW_TILE,),
        in_specs=[
            pl.BlockSpec((ROW_TILE, D_MODEL), lambda i: (i, 0)),
            pl.BlockSpec((None, 1, D_MODEL), lambda i: (l, 0, 0)),
            pl.BlockSpec((None, D_MODEL, D_IN), lambda i: (l, 0, 0), pipeline_mode=pl.Buffered(1)),
        ],
        out_specs=[
            pl.BlockSpec((ROW_TILE, ATTN_WIDTH), lambda i: (i, 0)),
            pl.BlockSpec((ROW_TILE, 2 * KV_WIDTH), lambda i: (i, 0)),
            pl.BlockSpec((ROW_TILE, SSM_WIDTH), lambda i: (i, 0)),
        ],
        out_shape=[
            jax.ShapeDtypeStruct((T_ALL, ATTN_WIDTH), BF16),
            jax.ShapeDtypeStruct((T_ALL, 2 * KV_WIDTH), F32),
            jax.ShapeDtypeStruct((T_ALL, SSM_WIDTH), F32),
        ],
        compiler_params=_cparams("parallel"),
        name="in_proj",
    )(x_all, g, w_in_b)


def _softmax_pv(s_parts, bias_parts, v_parts, sinks, rows):
    p_parts = [[] for _ in s_parts]
    dens = []
    for g in range(GQA_GROUP):
        sl = slice(g * rows, (g + 1) * rows)
        sk = sinks[g]
        sg = [s[sl] + b for s, b in zip(s_parts, bias_parts)]
        m = sk
        for s in sg:
            m = jnp.maximum(jnp.max(s, axis=-1, keepdims=True), m)
        den = jnp.exp(sk - m)
        for j, s in enumerate(sg):
            p = jnp.exp(s - m)
            den = den + jnp.sum(p, axis=-1, keepdims=True)
            p_parts[j].append(p.astype(BF16))
        dens.append(den)
    o = None
    for j, v in enumerate(v_parts):
        pj = jnp.concatenate(p_parts[j], axis=0)
        oj = jnp.dot(pj, v, preferred_element_type=F32)
        o = oj if o is None else o + oj
    return [o[g * rows:(g + 1) * rows] / dens[g] for g in range(GQA_GROUP)]


def _head_cols(h):
    return slice(h * HEAD_DIM, (h + 1) * HEAD_DIM)


def _stack_heads(q, hk):
    return jnp.concatenate([q[:, _head_cols(hk * GQA_GROUP + g)] for g in range(GQA_GROUP)], axis=0)


def _attn_prompt_block(n, sinks, q, kvc, kvp, gain):
    row = lax.broadcasted_iota(jnp.int32, (WINDOW, 2 * WINDOW), 0)
    col = lax.broadcasted_iota(jnp.int32, (WINDOW, 2 * WINDOW), 1)
    ok = (col > row) & (col <= row + WINDOW) & ((col >= WINDOW) | (n > 0))
    bias = jnp.where(ok, 0.0, MASK_NEG).astype(F32)
    outs = []
    for hk in range(N_KV_HEADS):
        ks, vs = _head_cols(hk), _head_cols(N_KV_HEADS + hk)
        k = jnp.concatenate([kvp[:, ks], kvc[:, ks]], axis=0).astype(BF16)
        v = jnp.concatenate([kvp[:, vs], kvc[:, vs]], axis=0).astype(BF16)
        s = lax.dot_general(_stack_heads(q, hk), k, _TRANS_B, preferred_element_type=F32)
        outs += _softmax_pv([s], [bias], [v], sinks[hk * GQA_GROUP:(hk + 1) * GQA_GROUP], WINDOW)
    return _rms(jnp.concatenate(outs, axis=1), gain).astype(BF16)


_SEQ_PER_CHUNK = 4
_ROWS_S = _SEQ_PER_CHUNK * DEC_SEQ


def _attn_sample_chunk(sinks, q, kv, kc, vc, gain):
    r = lax.broadcasted_iota(jnp.int32, (_ROWS_S, _SEQ_PER_CHUNK * WINDOW), 0)
    c = lax.broadcasted_iota(jnp.int32, (_ROWS_S, _SEQ_PER_CHUNK * WINDOW), 1)
    ok_c = (c // WINDOW == r // DEC_SEQ) & (c % WINDOW > r % DEC_SEQ)
    bias_c = jnp.where(ok_c, 0.0, MASK_NEG).astype(F32)
    r2 = lax.broadcasted_iota(jnp.int32, (_ROWS_S, _ROWS_S), 0)
    c2 = lax.broadcasted_iota(jnp.int32, (_ROWS_S, _ROWS_S), 1)
    ok_n = (c2 // DEC_SEQ == r2 // DEC_SEQ) & (c2 % DEC_SEQ <= r2 % DEC_SEQ)
    bias_n = jnp.where(ok_n, 0.0, MASK_NEG).astype(F32)
    outs = []
    for hk in range(N_KV_HEADS):
        ks, vs = _head_cols(hk), _head_cols(N_KV_HEADS + hk)
        k_c, v_c = kc[:, ks].astype(BF16), vc[:, ks].astype(BF16)
        k_n, v_n = kv[:, ks].astype(BF16), kv[:, vs].astype(BF16)
        qs = _stack_heads(q, hk)
        s_c = lax.dot_general(qs, k_c, _TRANS_B, preferred_element_type=F32)
        s_n = lax.dot_general(qs, k_n, _TRANS_B, preferred_element_type=F32)
        outs += _softmax_pv([s_c, s_n], [bias_c, bias_n], [v_c, v_n],
                            sinks[hk * GQA_GROUP:(hk + 1) * GQA_GROUP], _ROWS_S)
    return _rms(jnp.concatenate(outs, axis=1), gain).astype(BF16)


_N_PROMPT_BLOCKS = T_PROMPT // WINDOW


def _attn_kernel(sink_ref, q_ref, kvc_ref, kvp_ref, kc_ref, vc_ref, g_ref, o_ref):
    i = pl.program_id(0)
    sinks = [sink_ref[h] for h in range(N_HEADS)]
    gain = g_ref[...]

    @pl.when(i < _N_PROMPT_BLOCKS)
    def _():
        o_ref[...] = _attn_prompt_block(i % (SEQ // WINDOW), sinks, q_ref[...], kvc_ref[...], kvp_ref[...], gain)

    @pl.when(i == _N_PROMPT_BLOCKS)
    def _():
        def chunk(c, carry):
            rows = pl.ds(pl.multiple_of(c * _ROWS_S, _ROWS_S), _ROWS_S)
            seqs = pl.ds(c * _SEQ_PER_CHUNK, _SEQ_PER_CHUNK)
            kc = kc_ref[seqs].reshape(_SEQ_PER_CHUNK * WINDOW, KV_WIDTH)
            vc = vc_ref[seqs].reshape(_SEQ_PER_CHUNK * WINDOW, KV_WIDTH)
            o_ref[rows, :] = _attn_sample_chunk(sinks, q_ref[rows, :], kvc_ref[rows, :], kc, vc, gain)
            return carry
        lax.fori_loop(0, DEC_BATCH // _SEQ_PER_CHUNK, chunk, 0)


def _attention(sink, q_all, kv_all, kc, vc, g, l):
    nb = SEQ // WINDOW
    cache = pl.BlockSpec((None, DEC_BATCH, WINDOW, KV_WIDTH), lambda i: (l, 0, 0, 0), pipeline_mode=pl.Buffered(1))
    return pl.pallas_call(
        _attn_kernel,
        grid=(_N_PROMPT_BLOCKS + 1,),
        in_specs=[
            pl.BlockSpec(memory_space=pltpu.SMEM),
            pl.BlockSpec((WINDOW, ATTN_WIDTH), lambda i: (i, 0)),
            pl.BlockSpec((WINDOW, 2 * KV_WIDTH), lambda i: (i, 0)),
            pl.BlockSpec((WINDOW, 2 * KV_WIDTH), lambda i: (jnp.where(i % nb == 0, i, i - 1), 0)),
            cache,
            cache,
            pl.BlockSpec((None, 1, ATTN_WIDTH), lambda i: (l, 0, 0)),
        ],
        out_specs=pl.BlockSpec((WINDOW, ATTN_WIDTH), lambda i: (i, 0)),
        out_shape=jax.ShapeDtypeStruct((T_ALL, ATTN_WIDTH), BF16),
        compiler_params=_cparams("arbitrary"),
        name="attention",
    )(sink, q_all, kv_all, kv_all, kc, vc, g)


def _expand_groups(m):
    m2 = m.reshape(SLAB, SSM_STATE)
    t = jnp.concatenate([m2] * GROUPS_PER_SLAB, axis=1)
    r = lax.broadcasted_iota(jnp.int32, (SLAB, SLAB_STATE), 0) // SSM_GROUP
    c = lax.broadcasted_iota(jnp.int32, (SLAB, SLAB_STATE), 1) // SSM_STATE
    return jnp.where(r == c, t, 0.0)


def _fold_groups(x):
    t = [x[:, i * SLAB:(i + 1) * SLAB] for i in range(2 * SLAB_STATE // SLAB)]
    return jnp.concatenate([t[0] + t[1] + t[2] + t[3], t[4] + t[5] + t[6] + t[7]], axis=1)


def _ssm_prep_kernel(L, lam_ref, logdt_ref, bre_ref, bim_ref, cre_ref, cim_ref, tt_ref, bst_ref, cst_ref, al_ref):
    H = SLAB_STATE
    dt = jnp.exp(logdt_ref[...])
    lam = lam_ref[...]
    lr, li = lam[0:1], lam[1:2]
    n_tau = ((L + 1 + 7) // 8) * 8
    tau = lax.broadcasted_iota(jnp.int32, (n_tau, H), 0).astype(F32)
    mag = jnp.exp(tau * (lr * dt))
    ang = tau * (li * dt)
    pw_r, pw_i = mag * jnp.cos(ang), mag * jnp.sin(ang)
    abr, abi = pw_r[1:2], pw_i[1:2]
    nr = abr - 1.0
    den = lr * lr + li * li
    cr = (nr * lr + abi * li) / den
    ci = (abi * lr - nr * li) / den
    xb_r, xb_i = _expand_groups(bre_ref[...]), _expand_groups(bim_ref[...])
    xc_r, xc_i = _expand_groups(cre_ref[...]), _expand_groups(cim_ref[...])
    ccm = _fold_groups(jnp.concatenate([xc_r, -xc_i], axis=1))
    same_group = (lax.broadcasted_iota(jnp.int32, (SLAB, SLAB), 0) // SSM_GROUP
                  == lax.broadcasted_iota(jnp.int32, (SLAB, SLAB), 1) // SSM_GROUP)

    k_lag = [None] * L
    for s in range(L):
        e = L - 1 - s
        w_r = cr * pw_r[e:e + 1] - ci * pw_i[e:e + 1]
        w_i = cr * pw_i[e:e + 1] + ci * pw_r[e:e + 1]
        slab = jnp.concatenate([xb_r * w_r - xb_i * w_i, xb_r * w_i + xb_i * w_r], axis=1)
        bst_ref[s * SLAB:(s + 1) * SLAB, :] = slab.astype(BF16)
        k_all = lax.dot_general(_fold_groups(slab), ccm, _TRANS_B, preferred_element_type=F32, precision=_HI)
        k_lag[e] = jnp.where(same_group, k_all, 0.0)
    for s in range(L):
        for t_lo in range(2):
            lag = L - 2 + t_lo - s
            blk = k_lag[lag] if lag >= 0 else jnp.zeros((SLAB, SLAB), F32)
            tt_ref[s * SLAB:(s + 1) * SLAB, t_lo * SLAB:(t_lo + 1) * SLAB] = blk.astype(BF16)
    for t in range(L):
        p_r, p_i = pw_r[t + 1:t + 2], pw_i[t + 1:t + 2]
        g_t = jnp.concatenate([xc_r * p_r - xc_i * p_i, -(xc_r * p_i + xc_i * p_r)], axis=1)
        cst_ref[:, t * SLAB:(t + 1) * SLAB] = g_t.T.astype(BF16)
    al_ref[...] = jnp.concatenate([pw_r[L:L + 1], pw_i[L:L + 1]], axis=0)


def _ssm_prep(L, lam, logdt, bt_re, bt_im, c_re, c_im):
    H = SLAB_STATE
    q3 = lambda *blk: pl.BlockSpec((None,) + blk, lambda q: (q, 0, 0))
    grp = pl.BlockSpec((GROUPS_PER_SLAB, SSM_GROUP, SSM_STATE), lambda q: (q, 0, 0))
    return pl.pallas_call(
        functools.partial(_ssm_prep_kernel, L),
        grid=(N_SLABS,),
        in_specs=[q3(2, H), q3(1, H), grp, grp, grp, grp],
        out_specs=[q3(L * SLAB, 2 * SLAB), q3(L * SLAB, 2 * H), q3(2 * H, L * SLAB), q3(2, H)],
        out_shape=[
            jax.ShapeDtypeStruct((N_SLABS, L * SLAB, 2 * SLAB), BF16),
            jax.ShapeDtypeStruct((N_SLABS, L * SLAB, 2 * H), BF16),
            jax.ShapeDtypeStruct((N_SLABS, 2 * H, L * SLAB), BF16),
            jax.ShapeDtypeStruct((N_SLABS, 2, H), F32),
        ],
        compiler_params=_cparams("parallel"),
        name=f"ssm_prep_{L}",
    )(lam, logdt, bt_re, bt_im, c_re, c_im)


def _ssm_rows(u_ref, y_ref, row0, n_rows, L, ops, n_per_seq=None, h0=None):
    tt_ref, bst_ref, cst_ref, al_ref = ops
    H = SLAB_STATE
    nj = n_rows // L
    lhs = jnp.concatenate([u_ref[pl.ds(row0 + s, nj, stride=L), :].astype(BF16) for s in range(L)], axis=1)
    e = jnp.dot(lhs, bst_ref[...], preferred_element_type=F32)
    s_r, s_i = e[:, :H], e[:, H:]
    al = al_ref[...]
    a_r, a_i = al[0:1], al[1:2]
    if h0 is not None:
        in_r, in_i = h0
        s_r = s_r + in_r * a_r - in_i * a_i
        s_i = s_i + in_r * a_i + in_i * a_r
    else:
        jj = lax.broadcasted_iota(jnp.int32, (nj, H), 0) % n_per_seq
        d = 1
        while d < n_per_seq:
            keep = jj >= d
            sh_r = jnp.where(keep, pltpu.roll(s_r, d, axis=0), 0.0)
            sh_i = jnp.where(keep, pltpu.roll(s_i, d, axis=0), 0.0)
            s_r, s_i = s_r + sh_r * a_r - sh_i * a_i, s_i + sh_r * a_i + sh_i * a_r
            a_r, a_i = a_r * a_r - a_i * a_i, 2.0 * a_r * a_i
            d *= 2
        in_r = jnp.where(jj >= 1, pltpu.roll(s_r, 1, axis=0), 0.0)
        in_i = jnp.where(jj >= 1, pltpu.roll(s_i, 1, axis=0), 0.0)
    y_st = jnp.dot(jnp.concatenate([in_r, in_i], axis=1).astype(BF16), cst_ref[...], preferred_element_type=F32)
    for tp in range(L // 2):
        k = 2 * SLAB * (tp + 1)
        y = jnp.dot(lhs[:, :k], tt_ref[2 * SLAB * (L // 2 - 1 - tp):, :], preferred_element_type=F32)
        y = y + y_st[:, 2 * SLAB * tp:2 * SLAB * (tp + 1)]
        y_ref[pl.ds(row0 + 2 * tp, nj, stride=L), :] = y[:, :SLAB]
        y_ref[pl.ds(row0 + 2 * tp + 1, nj, stride=L), :] = y[:, SLAB:]
    return s_r, s_i


def _ssm_kernel(u_ref, h0r_ref, h0i_ref, tt_p, bst_p, cst_p, al_p, tt_s, bst_s, cst_s, al_s,
                y_ref, pr_ref, pi_ref, sr_ref, si_ref):
    n_per_seq = SEQ // CHUNK_P
    s_r, s_i = _ssm_rows(u_ref, y_ref, 0, T_PROMPT, CHUNK_P, (tt_p, bst_p, cst_p, al_p), n_per_seq=n_per_seq)
    for b in range(BATCH):
        last = (b + 1) * n_per_seq - 1
        pr_ref[b:b + 1, :] = s_r[last:last + 1]
        pi_ref[b:b + 1, :] = s_i[last:last + 1]
    n_r, n_i = _ssm_rows(u_ref, y_ref, T_PROMPT, T_SAMPLE, CHUNK_S, (tt_s, bst_s, cst_s, al_s),
                         h0=(h0r_ref[...], h0i_ref[...]))
    sr_ref[...] = n_r
    si_ref[...] = n_i


def _ssm_operator_specs(L):
    H = SLAB_STATE
    q3 = lambda *blk: pl.BlockSpec((None,) + blk, lambda q: (q, 0, 0))
    return [q3(L * SLAB, 2 * SLAB), q3(L * SLAB, 2 * H), q3(2 * H, L * SLAB), q3(2, H)]


def _ssm(u_all, h0_r, h0_i, ops_p, ops_s):
    H = SLAB_STATE
    slab = lambda rows: pl.BlockSpec((rows, SLAB), lambda q: (0, q))
    state = lambda rows: pl.BlockSpec((rows, H), lambda q: (0, q))
    return pl.pallas_call(
        _ssm_kernel,
        grid=(N_SLABS,),
        in_specs=[slab(T_ALL), state(DEC_BATCH), state(DEC_BATCH)]
        + _ssm_operator_specs(CHUNK_P) + _ssm_operator_specs(CHUNK_S),
        out_specs=[slab(T_ALL), state(BATCH), state(BATCH), state(DEC_BATCH), state(DEC_BATCH)],
        out_shape=[
            jax.ShapeDtypeStruct((T_ALL, SSM_WIDTH), F32),
            jax.ShapeDtypeStruct((BATCH, N_SLABS * H), F32),
            jax.ShapeDtypeStruct((BATCH, N_SLABS * H), F32),
            jax.ShapeDtypeStruct((DEC_BATCH, N_SLABS * H), F32),
            jax.ShapeDtypeStruct((DEC_BATCH, N_SLABS * H), F32),
        ],
        compiler_params=_cparams("parallel"),
        name="ssm",
    )(u_all, h0_r, h0_i, *ops_p, *ops_s)


def _glu_kernel(y_ref, u_ref, d_ref, w_ref, g_ref, o_ref):
    y = y_ref[...] + d_ref[...] * u_ref[...]
    z = 0.5 * y * (1.0 + lax.erf(y * (1.0 / math.sqrt(2.0))))
    gate = jnp.dot(z.astype(BF16), w_ref[...], preferred_element_type=F32)
    out = z * jax.nn.sigmoid(gate)
    o_ref[...] = _rms(out, g_ref[...]).astype(BF16)


def _glu(y_all, u_all, d, w_glu_b, g, l):
    row = lambda i: (i, 0)
    par = lambda i: (l, 0, 0)
    return pl.pallas_call(
        _glu_kernel,
        grid=(T_ALL // ROW_TILE,),
        in_specs=[
            pl.BlockSpec((ROW_TILE, SSM_WIDTH), row),
            pl.BlockSpec((ROW_TILE, SSM_WIDTH), row),
            pl.BlockSpec((None, 1, SSM_WIDTH), par),
            pl.BlockSpec((None, SSM_WIDTH, SSM_WIDTH), par, pipeline_mode=pl.Buffered(1)),
            pl.BlockSpec((None, 1, SSM_WIDTH), par),
        ],
        out_specs=pl.BlockSpec((ROW_TILE, SSM_WIDTH), row),
        out_shape=jax.ShapeDtypeStruct((T_ALL, SSM_WIDTH), BF16),
        compiler_params=_cparams("parallel"),
        name="glu",
    )(y_all, u_all, d, w_glu_b, g)


def _out_proj_kernel(a_ref, s_ref, w_ref, x_ref, o_ref):
    mixed = jnp.dot(jnp.concatenate([a_ref[...], s_ref[...]], axis=1), w_ref[...], preferred_element_type=F32)
    o_ref[...] = x_ref[...] + mixed


def _out_proj(a_n, s_n, w_out_b, x_all, l):
    row = lambda i: (i, 0)
    return pl.pallas_call(
        _out_proj_kernel,
        grid=(T_ALL // ROW_TILE,),
        in_specs=[
            pl.BlockSpec((ROW_TILE, ATTN_WIDTH), row),
            pl.BlockSpec((ROW_TILE, SSM_WIDTH), row),
            pl.BlockSpec((None, D_MODEL, D_MODEL), lambda i: (l, 0, 0), pipeline_mode=pl.Buffered(1)),
            pl.BlockSpec((ROW_TILE, D_MODEL), row),
        ],
        out_specs=pl.BlockSpec((ROW_TILE, D_MODEL), row),
        out_shape=jax.ShapeDtypeStruct((T_ALL, D_MODEL), F32),
        compiler_params=_cparams("parallel"),
        name="out_proj",
    )(a_n, s_n, w_out_b, x_all)


def _ffn_kernel(x_ref, g_ref, wg_ref, wu_ref, wd_ref, o_ref, h_ref):
    @pl.when(pl.program_id(1) == 0)
    def _():
        x = x_ref[...]
        h_ref[...] = _rms(x, g_ref[...]).astype(BF16)
        o_ref[...] = x

    h = h_ref[...]
    gate = jnp.dot(h, wg_ref[...], preferred_element_type=F32)
    up = jnp.dot(h, wu_ref[...], preferred_element_type=F32)
    act = (gate * jax.nn.sigmoid(gate) * up).astype(BF16)
    o_ref[...] += jnp.dot(act, wd_ref[...], preferred_element_type=F32)


def _ffn(x_all, g, w_gate_b, w_up_b, w_down_b, l):
    return pl.pallas_call(
        _ffn_kernel,
        grid=(T_ALL // ROW_TILE, D_FF // FF_TILE),
        in_specs=[
            pl.BlockSpec((ROW_TILE, D_MODEL), lambda i, f: (i, 0)),
            pl.BlockSpec((None, 1, D_MODEL), lambda i, f: (l, 0, 0)),
            pl.BlockSpec((None, D_MODEL, FF_TILE), lambda i, f: (l, 0, f)),
            pl.BlockSpec((None, D_MODEL, FF_TILE), lambda i, f: (l, 0, f)),
            pl.BlockSpec((None, FF_TILE, D_MODEL), lambda i, f: (l, f, 0)),
        ],
        out_specs=pl.BlockSpec((ROW_TILE, D_MODEL), lambda i, f: (i, 0)),
        out_shape=jax.ShapeDtypeStruct((T_ALL, D_MODEL), F32),
        scratch_shapes=[pltpu.VMEM((ROW_TILE, D_MODEL), BF16)],
        compiler_params=_cparams("parallel", "arbitrary"),
        name="ffn",
    )(x_all, g, w_gate_b, w_up_b, w_down_b)


def _norm_kernel(x_ref, g_ref, o_ref):
    o_ref[...] = _rms(x_ref[...], g_ref[...])


def _final_norm(x_all, g, rows, tile, first_block):
    return pl.pallas_call(
        _norm_kernel,
        grid=(rows // tile,),
        in_specs=[
            pl.BlockSpec((tile, D_MODEL), lambda i: (first_block + i, 0)),
            pl.BlockSpec((1, D_MODEL), lambda i: (0, 0)),
        ],
        out_specs=pl.BlockSpec((tile, D_MODEL), lambda i: (i, 0)),
        out_shape=jax.ShapeDtypeStruct((rows, D_MODEL), F32),
        compiler_params=_cparams("parallel"),
        name="final_norm",
    )(x_all, g)


def kernel(x_prompt, x_sample, cache_k, cache_v, state_ssm_re, state_ssm_im, norm_mix, w_in, attn_sink, ssm_a_re,
           ssm_a_im, ssm_log_dt, ssm_b_re, ssm_b_im, ssm_c_re, ssm_c_im, ssm_d, w_glu, norm_attn_out, norm_ssm_out,
           w_out, norm_ffn, w_gate, w_up, w_down, norm_final):
    G, P, H = N_SSM_GROUPS, SSM_STATE, SLAB_STATE
    x_all = jnp.concatenate([x_prompt.reshape(T_PROMPT, D_MODEL), x_sample.reshape(T_SAMPLE, D_MODEL)], axis=0)
    w_in_b, w_glu_b, w_out_b = w_in.astype(BF16), w_glu.astype(BF16), w_out.astype(BF16)
    w_gate_b, w_up_b, w_down_b = w_gate.astype(BF16), w_up.astype(BF16), w_down.astype(BF16)
    row3 = lambda p: p.reshape(DEPTH, 1, -1)
    g_mix, g_attn, g_ssm, g_ffn, d3 = row3(norm_mix), row3(norm_attn_out), row3(norm_ssm_out), row3(norm_ffn), row3(ssm_d)
    cache_k2 = cache_k.reshape(DEPTH, DEC_BATCH, WINDOW, KV_WIDTH)
    cache_v2 = cache_v.reshape(DEPTH, DEC_BATCH, WINDOW, KV_WIDTH)

    k_p, v_p, hr_p, hi_p, k_s, v_s, hr_s, hi_s = [], [], [], [], [], [], [], []
    for l in range(DEPTH):
        q_all, kv_all, u_all = _in_proj(x_all, g_mix, w_in_b, l)

        a_n = _attention(attn_sink[l], q_all, kv_all, cache_k2, cache_v2, g_attn, l)

        lam = jnp.stack([ssm_a_re[l].reshape(N_SLABS, H), ssm_a_im[l].reshape(N_SLABS, H)], axis=1)
        logdt = jnp.repeat(ssm_log_dt[l], P).reshape(N_SLABS, 1, H)
        prep_args = (lam, logdt, ssm_b_re[l].transpose(0, 2, 1), ssm_b_im[l].transpose(0, 2, 1),
                     ssm_c_re[l], ssm_c_im[l])
        ops_p = _ssm_prep(CHUNK_P, *prep_args)
        ops_s = _ssm_prep(CHUNK_S, *prep_args)
        y_all, sfin_r, sfin_i, snew_r, snew_i = _ssm(u_all, state_ssm_re[l].reshape(DEC_BATCH, G * P),
                                                     state_ssm_im[l].reshape(DEC_BATCH, G * P), ops_p, ops_s)
        s_n = _glu(y_all, u_all, d3, w_glu_b, g_ssm, l)

        x_all = _out_proj(a_n, s_n, w_out_b, x_all, l)
        x_all = _ffn(x_all, g_ffn, w_gate_b, w_up_b, w_down_b, l)

        kv_p = kv_all[:T_PROMPT].reshape(BATCH, SEQ, 2, N_KV_HEADS, HEAD_DIM)[:, -WINDOW:]
        k_p.append(kv_p[:, :, 0])
        v_p.append(kv_p[:, :, 1])
        kv_s = kv_all[T_PROMPT:].reshape(DEC_BATCH, DEC_SEQ, 2, N_KV_HEADS, HEAD_DIM)
        k_s.append(jnp.concatenate([cache_k[l][:, DEC_SEQ:], kv_s[:, :, 0]], axis=1))
        v_s.append(jnp.concatenate([cache_v[l][:, DEC_SEQ:], kv_s[:, :, 1]], axis=1))
        hr_p.append(sfin_r.reshape(BATCH, G, P))
        hi_p.append(sfin_i.reshape(BATCH, G, P))
        hr_s.append(snew_r.reshape(DEC_BATCH, G, P))
        hi_s.append(snew_i.reshape(DEC_BATCH, G, P))

    g_fin = norm_final.reshape(1, D_MODEL)
    y_prompt = _final_norm(x_all, g_fin, T_PROMPT, 1024, 0).reshape(BATCH, SEQ, D_MODEL)
    y_sample = _final_norm(x_all, g_fin, T_SAMPLE, T_SAMPLE, T_PROMPT // T_SAMPLE).reshape(DEC_BATCH, DEC_SEQ, D_MODEL)
    st = jnp.stack
    return (y_prompt, y_sample, st(k_p), st(v_p), st(hr_p), st(hi_p), st(k_s), st(v_s), st(hr_s), st(hi_s))
```

```python
import math

import jax
import jax.numpy as jnp
from jax import lax
from jax.experimental import pallas as pl
from jax.experimental.pallas import tpu as pltpu

F32 = jnp.float32
BF16 = jnp.bfloat16

D_MODEL = 2048
BATCH = 2
SEQ = 4096
DEPTH = 4
DEC_BATCH = 32
DEC_SEQ = 4
ATTN_WIDTH = 1024
SSM_WIDTH = 1024
HEAD_DIM = 64
N_HEADS = 16
N_KV_HEADS = 2
GQA_GROUP = 8
KV_WIDTH = 128
WINDOW = 128
SSM_GROUP = 16
N_SSM_GROUPS = 64
SSM_STATE = 64
D_IN = ATTN_WIDTH + 2 * KV_WIDTH + SSM_WIDTH
D_FF = 5632
EPS = 1e-5

T_PROMPT = BATCH * SEQ
T_SAMPLE = DEC_BATCH * DEC_SEQ
T_ALL = T_PROMPT + T_SAMPLE
ROW_TILE = 1040
FF_TILE = 512
CHUNK_P = 16
CHUNK_S = DEC_SEQ
SLAB = 128
GROUPS_PER_SLAB = SLAB // SSM_GROUP
N_SLABS = SSM_WIDTH // SLAB
SLAB_STATE = GROUPS_PER_SLAB * SSM_STATE
MASK_NEG = -1e30
VMEM_LIMIT = 56 * 1024 * 1024

_TRANS_B = (((1,), (1,)), ((), ()))
_HI = lax.Precision.HIGHEST


def _cparams(*sem):
    return pltpu.CompilerParams(dimension_semantics=sem, vmem_limit_bytes=VMEM_LIMIT)


def _rms(x, g):
    ms = jnp.mean(x * x, axis=-1, keepdims=True)
    return x * lax.rsqrt(ms + EPS) * g


def _in_proj_kernel(x_ref, g_ref, w_ref, q_ref, kv_ref, u_ref):
    h = _rms(x_ref[...], g_ref[...]).astype(BF16)
    z = jnp.dot(h, w_ref[...], preferred_element_type=F32)
    q_ref[...] = (z[:, :ATTN_WIDTH] * (HEAD_DIM ** -0.5)).astype(BF16)
    kv_ref[...] = z[:, ATTN_WIDTH:ATTN_WIDTH + 2 * KV_WIDTH]
    u_ref[...] = z[:, ATTN_WIDTH + 2 * KV_WIDTH:]


def _in_proj_out(tile):
    row = lambda i: (i, 0)
    specs = [pl.BlockSpec((tile, ATTN_WIDTH), row), pl.BlockSpec((tile, 2 * KV_WIDTH), row),
             pl.BlockSpec((tile, SSM_WIDTH), row)]
    shapes = [jax.ShapeDtypeStruct((T_ALL, ATTN_WIDTH), BF16), jax.ShapeDtypeStruct((T_ALL, 2 * KV_WIDTH), F32),
              jax.ShapeDtypeStruct((T_ALL, SSM_WIDTH), F32)]
    return specs, shapes


def _in_proj(x_all, g, w_in_b, l):
    out_specs, out_shapes = _in_proj_out(ROW_TILE)
    return pl.pallas_call(
        _in_proj_kernel,
        grid=(T_ALL // ROW_TILE,),
        in_specs=[
            pl.BlockSpec((ROW_TILE, D_MODEL), lambda i: (i, 0)),
            pl.BlockSpec((None, 1, D_MODEL), lambda i: (l, 0, 0)),
            pl.BlockSpec((None, D_MODEL, D_IN), lambda i: (l, 0, 0), pipeline_mode=pl.Buffered(1)),
        ],
        out_specs=out_specs,
        out_shape=out_shapes,
        compiler_params=_cparams("parallel"),
        name="in_proj",
    )(x_all, g, w_in_b)


FIRST_TILE = 640
_LAST_TILE = T_ALL // FIRST_TILE - 1
_PROMPT_ROWS_LAST_TILE = T_PROMPT - _LAST_TILE * FIRST_TILE


def _in_proj_first_kernel(xp_ref, xs_ref, g_ref, w_ref, q_ref, kv_ref, u_ref, x_ref):
    i = pl.program_id(0)

    @pl.when(i < _LAST_TILE)
    def _():
        x_ref[...] = xp_ref[...]

    @pl.when(i == _LAST_TILE)
    def _():
        x_ref[:_PROMPT_ROWS_LAST_TILE, :] = xp_ref[:_PROMPT_ROWS_LAST_TILE, :]
        x_ref[_PROMPT_ROWS_LAST_TILE:, :] = xs_ref[...]

    _in_proj_kernel(x_ref, g_ref, w_ref, q_ref, kv_ref, u_ref)


def _in_proj_first(x_prompt, x_sample, g, w_in_b):
    assert FIRST_TILE - _PROMPT_ROWS_LAST_TILE == T_SAMPLE
    out_specs, out_shapes = _in_proj_out(FIRST_TILE)
    return pl.pallas_call(
        _in_proj_first_kernel,
        grid=(T_ALL // FIRST_TILE,),
        in_specs=[
            pl.BlockSpec((FIRST_TILE, D_MODEL), lambda i: (i, 0)),
            pl.BlockSpec((T_SAMPLE, D_MODEL), lambda i: (0, 0)),
            pl.BlockSpec((None, 1, D_MODEL), lambda i: (0, 0, 0)),
            pl.BlockSpec((None, D_MODEL, D_IN), lambda i: (0, 0, 0), pipeline_mode=pl.Buffered(1)),
        ],
        out_specs=out_specs + [pl.BlockSpec((FIRST_TILE, D_MODEL), lambda i: (i, 0))],
        out_shape=out_shapes + [jax.ShapeDtypeStruct((T_ALL, D_MODEL), F32)],
        compiler_params=_cparams("parallel"),
        name="in_proj_first",
    )(x_prompt, x_sample, g, w_in_b)


def _softmax_pv(s_parts, bias_parts, v_parts, sinks, rows):
    p_parts = [[] for _ in s_parts]
    dens = []
    for g in range(GQA_GROUP):
        sl = slice(g * rows, (g + 1) * rows)
        sk = sinks[g]
        sg = [s[sl] + b for s, b in zip(s_parts, bias_parts)]
        m = sk
        for s in sg:
            m = jnp.maximum(jnp.max(s, axis=-1, keepdims=True), m)
        den = jnp.exp(sk - m)
        for j, s in enumerate(sg):
            p = jnp.exp(s - m)
            den = den + jnp.sum(p, axis=-1, keepdims=True)
            p_parts[j].append(p.astype(BF16))
        dens.append(den)
    o = None
    for j, v in enumerate(v_parts):
        pj = jnp.concatenate(p_parts[j], axis=0)
        oj = jnp.dot(pj, v, preferred_element_type=F32)
        o = oj if o is None else o + oj
    return [o[g * rows:(g + 1) * rows] / dens[g] for g in range(GQA_GROUP)]


def _head_cols(h):
    return slice(h * HEAD_DIM, (h + 1) * HEAD_DIM)


def _stack_heads(q, hk):
    return jnp.concatenate([q[:, _head_cols(hk * GQA_GROUP + g)] for g in range(GQA_GROUP)], axis=0)


def _attn_prompt_block(n, sinks, q, kvc, kvp, gain):
    row = lax.broadcasted_iota(jnp.int32, (WINDOW, 2 * WINDOW), 0)
    col = lax.broadcasted_iota(jnp.int32, (WINDOW, 2 * WINDOW), 1)
    ok = (col > row) & (col <= row + WINDOW) & ((col >= WINDOW) | (n > 0))
    bias = jnp.where(ok, 0.0, MASK_NEG).astype(F32)
    outs = []
    for hk in range(N_KV_HEADS):
        ks, vs = _head_cols(hk), _head_cols(N_KV_HEADS + hk)
        k = jnp.concatenate([kvp[:, ks], kvc[:, ks]], axis=0).astype(BF16)
        v = jnp.concatenate([kvp[:, vs], kvc[:, vs]], axis=0).astype(BF16)
        for g in range(GQA_GROUP):
            h = hk * GQA_GROUP + g
            s = lax.dot_general(q[:, _head_cols(h)], k, _TRANS_B, preferred_element_type=F32) + bias
            m = jnp.maximum(jnp.max(s, axis=-1, keepdims=True), sinks[h])
            p = jnp.exp(s - m)
            den = jnp.sum(p, axis=-1, keepdims=True) + jnp.exp(sinks[h] - m)
            outs.append(jnp.dot(p.astype(BF16), v, preferred_element_type=F32) / den)
    return _rms(jnp.concatenate(outs, axis=1), gain).astype(BF16)


_SEQ_PER_CHUNK = 4
_ROWS_S = _SEQ_PER_CHUNK * DEC_SEQ


def _attn_sample_chunk(sinks, q, kv, kc, vc, gain):
    r = lax.broadcasted_iota(jnp.int32, (_ROWS_S, _SEQ_PER_CHUNK * WINDOW), 0)
    c = lax.broadcasted_iota(jnp.int32, (_ROWS_S, _SEQ_PER_CHUNK * WINDOW), 1)
    ok_c = (c // WINDOW == r // DEC_SEQ) & (c % WINDOW > r % DEC_SEQ)
    bias_c = jnp.where(ok_c, 0.0, MASK_NEG).astype(F32)
    r2 = lax.broadcasted_iota(jnp.int32, (_ROWS_S, _ROWS_S), 0)
    c2 = lax.broadcasted_iota(jnp.int32, (_ROWS_S, _ROWS_S), 1)
    ok_n = (c2 // DEC_SEQ == r2 // DEC_SEQ) & (c2 % DEC_SEQ <= r2 % DEC_SEQ)
    bias_n = jnp.where(ok_n, 0.0, MASK_NEG).astype(F32)
    outs = []
    for hk in range(N_KV_HEADS):
        ks, vs = _head_cols(hk), _head_cols(N_KV_HEADS + hk)
        k_c, v_c = kc[:, ks].astype(BF16), vc[:, ks].astype(BF16)
        k_n, v_n = kv[:, ks].astype(BF16), kv[:, vs].astype(BF16)
        qs = _stack_heads(q, hk)
        s_c = lax.dot_general(qs, k_c, _TRANS_B, preferred_element_type=F32)
        s_n = lax.dot_general(qs, k_n, _TRANS_B, preferred_element_type=F32)
        outs += _softmax_pv([s_c, s_n], [bias_c, bias_n], [v_c, v_n],
                            sinks[hk * GQA_GROUP:(hk + 1) * GQA_GROUP], _ROWS_S)
    return _rms(jnp.concatenate(outs, axis=1), gain).astype(BF16)


_N_PROMPT_BLOCKS = T_PROMPT // WINDOW


def _attn_kernel(sink_ref, q_ref, kvc_ref, kvp_ref, kc_ref, vc_ref, g_ref, o_ref):
    i = pl.program_id(0)
    sinks = [sink_ref[h] for h in range(N_HEADS)]
    gain = g_ref[...]

    @pl.when(i < _N_PROMPT_BLOCKS)
    def _():
        o_ref[...] = _attn_prompt_block(i % (SEQ // WINDOW), sinks, q_ref[...], kvc_ref[...], kvp_ref[...], gain)

    @pl.when(i == _N_PROMPT_BLOCKS)
    def _():
        def chunk(c, carry):
            rows = pl.ds(pl.multiple_of(c * _ROWS_S, _ROWS_S), _ROWS_S)
            seqs = pl.ds(c * _SEQ_PER_CHUNK, _SEQ_PER_CHUNK)
            kc = kc_ref[seqs].reshape(_SEQ_PER_CHUNK * WINDOW, KV_WIDTH)
            vc = vc_ref[seqs].reshape(_SEQ_PER_CHUNK * WINDOW, KV_WIDTH)
            o_ref[rows, :] = _attn_sample_chunk(sinks, q_ref[rows, :], kvc_ref[rows, :], kc, vc, gain)
            return carry
        lax.fori_loop(0, DEC_BATCH // _SEQ_PER_CHUNK, chunk, 0)


def _attention(sink, q_all, kv_all, kc, vc, g, l):
    nb = SEQ // WINDOW
    cache = pl.BlockSpec((None, DEC_BATCH, WINDOW, KV_WIDTH), lambda i: (l, 0, 0, 0), pipeline_mode=pl.Buffered(1))
    return pl.pallas_call(
        _attn_kernel,
        grid=(_N_PROMPT_BLOCKS + 1,),
        in_specs=[
            pl.BlockSpec(memory_space=pltpu.SMEM),
            pl.BlockSpec((WINDOW, ATTN_WIDTH), lambda i: (i, 0)),
            pl.BlockSpec((WINDOW, 2 * KV_WIDTH), lambda i: (i, 0)),
            pl.BlockSpec((WINDOW, 2 * KV_WIDTH), lambda i: (jnp.where(i % nb == 0, i, i - 1), 0)),
            cache,
            cache,
            pl.BlockSpec((None, 1, ATTN_WIDTH), lambda i: (l, 0, 0)),
        ],
        out_specs=pl.BlockSpec((WINDOW, ATTN_WIDTH), lambda i: (i, 0)),
        out_shape=jax.ShapeDtypeStruct((T_ALL, ATTN_WIDTH), BF16),
        compiler_params=_cparams("arbitrary"),
        name="attention",
    )(sink, q_all, kv_all, kv_all, kc, vc, g)


def _expand_groups(m):
    m2 = m.reshape(SLAB, SSM_STATE)
    t = jnp.concatenate([m2] * GROUPS_PER_SLAB, axis=1)
    r = lax.broadcasted_iota(jnp.int32, (SLAB, SLAB_STATE), 0) // SSM_GROUP
    c = lax.broadcasted_iota(jnp.int32, (SLAB, SLAB_STATE), 1) // SSM_STATE
    return jnp.where(r == c, t, 0.0)


def _fold_groups(x):
    t = [x[:, i * SLAB:(i + 1) * SLAB] for i in range(2 * SLAB_STATE // SLAB)]
    return jnp.concatenate([t[0] + t[1] + t[2] + t[3], t[4] + t[5] + t[6] + t[7]], axis=1)


def _ssm_tables(lam_ref, logdt_ref, bre_ref, bim_ref, cre_ref, cim_ref, max_chunk):
    H = SLAB_STATE
    dt = jnp.exp(logdt_ref[...])
    lam = lam_ref[...]
    lr, li = lam[0:1], lam[1:2]
    n_tau = ((max_chunk + 1 + 7) // 8) * 8
    tau = lax.broadcasted_iota(jnp.int32, (n_tau, H), 0).astype(F32)
    mag = jnp.exp(tau * (lr * dt))
    ang = tau * (li * dt)
    pw_r, pw_i = mag * jnp.cos(ang), mag * jnp.sin(ang)
    abr, abi = pw_r[1:2], pw_i[1:2]
    nr = abr - 1.0
    den = lr * lr + li * li
    cr = (nr * lr + abi * li) / den
    ci = (abi * lr - nr * li) / den
    xb_r, xb_i = _expand_groups(bre_ref[...]), _expand_groups(bim_ref[...])
    xc_r, xc_i = _expand_groups(cre_ref[...]), _expand_groups(cim_ref[...])
    return pw_r, pw_i, cr, ci, xb_r, xb_i, xc_r, xc_i


def _ssm_operators(L, tables, tt_ref, bst_ref, cst_ref):
    pw_r, pw_i, cr, ci, xb_r, xb_i, xc_r, xc_i = tables
    ccm = _fold_groups(jnp.concatenate([xc_r, -xc_i], axis=1))
    same_group = (lax.broadcasted_iota(jnp.int32, (SLAB, SLAB), 0) // SSM_GROUP
                  == lax.broadcasted_iota(jnp.int32, (SLAB, SLAB), 1) // SSM_GROUP)

    k_lag = [None] * L
    for s in range(L):
        e = L - 1 - s
        w_r = cr * pw_r[e:e + 1] - ci * pw_i[e:e + 1]
        w_i = cr * pw_i[e:e + 1] + ci * pw_r[e:e + 1]
        slab = jnp.concatenate([xb_r * w_r - xb_i * w_i, xb_r * w_i + xb_i * w_r], axis=1)
        bst_ref[s * SLAB:(s + 1) * SLAB, :] = slab.astype(BF16)
        k_all = lax.dot_general(_fold_groups(slab), ccm, _TRANS_B, preferred_element_type=F32, precision=_HI)
        k_lag[e] = jnp.where(same_group, k_all, 0.0)
    for s in range(L):
        for t_lo in range(2):
            lag = L - 2 + t_lo - s
            blk = k_lag[lag] if lag >= 0 else jnp.zeros((SLAB, SLAB), F32)
            tt_ref[s * SLAB:(s + 1) * SLAB, t_lo * SLAB:(t_lo + 1) * SLAB] = blk.astype(BF16)
    for t in range(L):
        p_r, p_i = pw_r[t + 1:t + 2], pw_i[t + 1:t + 2]
        g_t = jnp.concatenate([xc_r * p_r - xc_i * p_i, -(xc_r * p_i + xc_i * p_r)], axis=1)
        cst_ref[:, t * SLAB:(t + 1) * SLAB] = g_t.T.astype(BF16)
    return pw_r[L:L + 1], pw_i[L:L + 1]


def _ssm_rows(u_ref, y_ref, row0, n_rows, L, ops, n_per_seq=None, h0=None):
    tt_ref, bst_ref, cst_ref, (a_r, a_i) = ops
    H = SLAB_STATE
    nj = n_rows // L
    lhs = jnp.concatenate([u_ref[pl.ds(row0 + s, nj, stride=L), :].astype(BF16) for s in range(L)], axis=1)
    e = jnp.dot(lhs, bst_ref[...], preferred_element_type=F32)
    s_r, s_i = e[:, :H], e[:, H:]
    if h0 is not None:
        in_r, in_i = h0
        s_r = s_r + in_r * a_r - in_i * a_i
        s_i = s_i + in_r * a_i + in_i * a_r
    else:
        jj = lax.broadcasted_iota(jnp.int32, (nj, H), 0) % n_per_seq
        d = 1
        while d < n_per_seq:
            keep = jj >= d
            sh_r = jnp.where(keep, pltpu.roll(s_r, d, axis=0), 0.0)
            sh_i = jnp.where(keep, pltpu.roll(s_i, d, axis=0), 0.0)
            s_r, s_i = s_r + sh_r * a_r - sh_i * a_i, s_i + sh_r * a_i + sh_i * a_r
            a_r, a_i = a_r * a_r - a_i * a_i, 2.0 * a_r * a_i
            d *= 2
        in_r = jnp.where(jj >= 1, pltpu.roll(s_r, 1, axis=0), 0.0)
        in_i = jnp.where(jj >= 1, pltpu.roll(s_i, 1, axis=0), 0.0)
    y_st = jnp.dot(jnp.concatenate([in_r, in_i], axis=1).astype(BF16), cst_ref[...], preferred_element_type=F32)
    for tp in range(L // 2):
        k = 2 * SLAB * (tp + 1)
        y = jnp.dot(lhs[:, :k], tt_ref[2 * SLAB * (L // 2 - 1 - tp):, :], preferred_element_type=F32)
        y = y + y_st[:, 2 * SLAB * tp:2 * SLAB * (tp + 1)]
        y_ref[pl.ds(row0 + 2 * tp, nj, stride=L), :] = y[:, :SLAB]
        y_ref[pl.ds(row0 + 2 * tp + 1, nj, stride=L), :] = y[:, SLAB:]
    return s_r, s_i


def _ssm_kernel(u_ref, h0r_ref, h0i_ref, lam_ref, logdt_ref, bre_ref, bim_ref, cre_ref, cim_ref,
                y_ref, pr_ref, pi_ref, sr_ref, si_ref,
                tt_p, bst_p, cst_p, tt_s, bst_s, cst_s):
    tables = _ssm_tables(lam_ref, logdt_ref, bre_ref, bim_ref, cre_ref, cim_ref, max(CHUNK_P, CHUNK_S))
    al_p = _ssm_operators(CHUNK_P, tables, tt_p, bst_p, cst_p)
    al_s = _ssm_operators(CHUNK_S, tables, tt_s, bst_s, cst_s)
    n_per_seq = SEQ // CHUNK_P
    s_r, s_i = _ssm_rows(u_ref, y_ref, 0, T_PROMPT, CHUNK_P, (tt_p, bst_p, cst_p, al_p), n_per_seq=n_per_seq)
    for b in range(BATCH):
        last = (b + 1) * n_per_seq - 1
        pr_ref[b:b + 1, :] = s_r[last:last + 1]
        pi_ref[b:b + 1, :] = s_i[last:last + 1]
    n_r, n_i = _ssm_rows(u_ref, y_ref, T_PROMPT, T_SAMPLE, CHUNK_S, (tt_s, bst_s, cst_s, al_s),
                         h0=(h0r_ref[...], h0i_ref[...]))
    sr_ref[...] = n_r
    si_ref[...] = n_i


def _ssm_operator_scratch(L):
    H = SLAB_STATE
    return [pltpu.VMEM((L * SLAB, 2 * SLAB), BF16), pltpu.VMEM((L * SLAB, 2 * H), BF16),
            pltpu.VMEM((2 * H, L * SLAB), BF16)]


def _ssm(u_all, h0_r, h0_i, lam, logdt, bt_re, bt_im, c_re, c_im):
    H = SLAB_STATE
    slab = lambda rows: pl.BlockSpec((rows, SLAB), lambda q: (0, q))
    state = lambda rows: pl.BlockSpec((rows, H), lambda q: (0, q))
    q3 = lambda *blk: pl.BlockSpec((None,) + blk, lambda q: (q, 0, 0))
    grp = pl.BlockSpec((GROUPS_PER_SLAB, SSM_GROUP, SSM_STATE), lambda q: (q, 0, 0))
    return pl.pallas_call(
        _ssm_kernel,
        grid=(N_SLABS,),
        in_specs=[slab(T_ALL), state(DEC_BATCH), state(DEC_BATCH), q3(2, H), q3(1, H), grp, grp, grp, grp],
        out_specs=[slab(T_ALL), state(BATCH), state(BATCH), state(DEC_BATCH), state(DEC_BATCH)],
        out_shape=[
            jax.ShapeDtypeStruct((T_ALL, SSM_WIDTH), F32),
            jax.ShapeDtypeStruct((BATCH, N_SLABS * H), F32),
            jax.ShapeDtypeStruct((BATCH, N_SLABS * H), F32),
            jax.ShapeDtypeStruct((DEC_BATCH, N_SLABS * H), F32),
            jax.ShapeDtypeStruct((DEC_BATCH, N_SLABS * H), F32),
        ],
        scratch_shapes=_ssm_operator_scratch(CHUNK_P) + _ssm_operator_scratch(CHUNK_S),
        compiler_params=_cparams("parallel"),
        name="ssm",
    )(u_all, h0_r, h0_i, lam, logdt, bt_re, bt_im, c_re, c_im)


def _glu_kernel(y_ref, u_ref, d_ref, w_ref, g_ref, o_ref):
    y = y_ref[...] + d_ref[...] * u_ref[...]
    z = 0.5 * y * (1.0 + lax.erf(y * (1.0 / math.sqrt(2.0))))
    gate = jnp.dot(z.astype(BF16), w_ref[...], preferred_element_type=F32)
    out = z * jax.nn.sigmoid(gate)
    o_ref[...] = _rms(out, g_ref[...]).astype(BF16)


def _glu(y_all, u_all, d, w_glu_b, g, l):
    row = lambda i: (i, 0)
    par = lambda i: (l, 0, 0)
    return pl.pallas_call(
        _glu_kernel,
        grid=(T_ALL // ROW_TILE,),
        in_specs=[
            pl.BlockSpec((ROW_TILE, SSM_WIDTH), row),
            pl.BlockSpec((ROW_TILE, SSM_WIDTH), row),
            pl.BlockSpec((None, 1, SSM_WIDTH), par),
            pl.BlockSpec((None, SSM_WIDTH, SSM_WIDTH), par, pipeline_mode=pl.Buffered(1)),
            pl.BlockSpec((None, 1, SSM_WIDTH), par),
        ],
        out_specs=pl.BlockSpec((ROW_TILE, SSM_WIDTH), row),
        out_shape=jax.ShapeDtypeStruct((T_ALL, SSM_WIDTH), BF16),
        compiler_params=_cparams("parallel"),
        name="glu",
    )(y_all, u_all, d, w_glu_b, g)


def _out_proj_kernel(a_ref, s_ref, w_ref, x_ref, o_ref):
    mixed = jnp.dot(jnp.concatenate([a_ref[...], s_ref[...]], axis=1), w_ref[...], preferred_element_type=F32)
    o_ref[...] = x_ref[...] + mixed


def _out_proj(a_n, s_n, w_out_b, x_all, l):
    row = lambda i: (i, 0)
    return pl.pallas_call(
        _out_proj_kernel,
        grid=(T_ALL // ROW_TILE,),
        in_specs=[
            pl.BlockSpec((ROW_TILE, ATTN_WIDTH), row),
            pl.BlockSpec((ROW_TILE, SSM_WIDTH), row),
            pl.BlockSpec((None, D_MODEL, D_MODEL), lambda i: (l, 0, 0), pipeline_mode=pl.Buffered(1)),
            pl.BlockSpec((ROW_TILE, D_MODEL), row),
        ],
        out_specs=pl.BlockSpec((ROW_TILE, D_MODEL), row),
        out_shape=jax.ShapeDtypeStruct((T_ALL, D_MODEL), F32),
        compiler_params=_cparams("parallel"),
        name="out_proj",
    )(a_n, s_n, w_out_b, x_all)


def _ffn_kernel(x_ref, g_ref, wg_ref, wu_ref, wd_ref, o_ref, h_ref):
    @pl.when(pl.program_id(1) == 0)
    def _():
        x = x_ref[...]
        h_ref[...] = _rms(x, g_ref[...]).astype(BF16)
        o_ref[...] = x

    h = h_ref[...]
    gate = jnp.dot(h, wg_ref[...], preferred_element_type=F32)
    up = jnp.dot(h, wu_ref[...], preferred_element_type=F32)
    act = (gate * jax.nn.sigmoid(gate) * up).astype(BF16)
    o_ref[...] += jnp.dot(act, wd_ref[...], preferred_element_type=F32)


def _ffn(x_all, g, w_gate_b, w_up_b, w_down_b, l):
    return pl.pallas_call(
        _ffn_kernel,
        grid=(T_ALL // ROW_TILE, D_FF // FF_TILE),
        in_specs=[
            pl.BlockSpec((ROW_TILE, D_MODEL), lambda i, f: (i, 0)),
            pl.BlockSpec((None, 1, D_MODEL), lambda i, f: (l, 0, 0)),
            pl.BlockSpec((None, D_MODEL, FF_TILE), lambda i, f: (l, 0, f)),
            pl.BlockSpec((None, D_MODEL, FF_TILE), lambda i, f: (l, 0, f)),
            pl.BlockSpec((None, FF_TILE, D_MODEL), lambda i, f: (l, f, 0)),
        ],
        out_specs=pl.BlockSpec((ROW_TILE, D_MODEL), lambda i, f: (i, 0)),
        out_shape=jax.ShapeDtypeStruct((T_ALL, D_MODEL), F32),
        scratch_shapes=[pltpu.VMEM((ROW_TILE, D_MODEL), BF16)],
        compiler_params=_cparams("parallel", "arbitrary"),
        name="ffn",
    )(x_all, g, w_gate_b, w_up_b, w_down_b)


def _norm_kernel(x_ref, g_ref, o_ref):
    o_ref[...] = _rms(x_ref[...], g_ref[...])


def _final_norm(x_all, g, rows, tile, first_block):
    return pl.pallas_call(
        _norm_kernel,
        grid=(rows // tile,),
        in_specs=[
            pl.BlockSpec((tile, D_MODEL), lambda i: (first_block + i, 0)),
            pl.BlockSpec((1, D_MODEL), lambda i: (0, 0)),
        ],
        out_specs=pl.BlockSpec((tile, D_MODEL), lambda i: (i, 0)),
        out_shape=jax.ShapeDtypeStruct((rows, D_MODEL), F32),
        compiler_params=_cparams("parallel"),
        name="final_norm",
    )(x_all, g)


def kernel(x_prompt, x_sample, cache_k, cache_v, state_ssm_re, state_ssm_im, norm_mix, w_in, attn_sink, ssm_a_re,
           ssm_a_im, ssm_log_dt, ssm_b_re, ssm_b_im, ssm_c_re, ssm_c_im, ssm_d, w_glu, norm_attn_out, norm_ssm_out,
           w_out, norm_ffn, w_gate, w_up, w_down, norm_final):
    G, P, H = N_SSM_GROUPS, SSM_STATE, SLAB_STATE
    w_in_b, w_glu_b, w_out_b = w_in.astype(BF16), w_glu.astype(BF16), w_out.astype(BF16)
    w_gate_b, w_up_b, w_down_b = w_gate.astype(BF16), w_up.astype(BF16), w_down.astype(BF16)
    row3 = lambda p: p.reshape(DEPTH, 1, -1)
    g_mix, g_attn, g_ssm, g_ffn, d3 = row3(norm_mix), row3(norm_attn_out), row3(norm_ssm_out), row3(norm_ffn), row3(ssm_d)
    cache_k2 = cache_k.reshape(DEPTH, DEC_BATCH, WINDOW, KV_WIDTH)
    cache_v2 = cache_v.reshape(DEPTH, DEC_BATCH, WINDOW, KV_WIDTH)

    k_p, v_p, hr_p, hi_p, k_s, v_s, hr_s, hi_s = [], [], [], [], [], [], [], []
    for l in range(DEPTH):
        if l == 0:
            q_all, kv_all, u_all, x_all = _in_proj_first(x_prompt.reshape(T_PROMPT, D_MODEL),
                                                         x_sample.reshape(T_SAMPLE, D_MODEL), g_mix, w_in_b)
        else:
            q_all, kv_all, u_all = _in_proj(x_all, g_mix, w_in_b, l)

        a_n = _attention(attn_sink[l], q_all, kv_all, cache_k2, cache_v2, g_attn, l)

        lam = jnp.stack([ssm_a_re[l].reshape(N_SLABS, H), ssm_a_im[l].reshape(N_SLABS, H)], axis=1)
        logdt = jnp.repeat(ssm_log_dt[l], P).reshape(N_SLABS, 1, H)
        y_all, sfin_r, sfin_i, snew_r, snew_i = _ssm(
            u_all, state_ssm_re[l].reshape(DEC_BATCH, G * P), state_ssm_im[l].reshape(DEC_BATCH, G * P),
            lam, logdt, ssm_b_re[l].transpose(0, 2, 1), ssm_b_im[l].transpose(0, 2, 1), ssm_c_re[l], ssm_c_im[l])
        s_n = _glu(y_all, u_all, d3, w_glu_b, g_ssm, l)

        x_all = _out_proj(a_n, s_n, w_out_b, x_all, l)
        x_all = _ffn(x_all, g_ffn, w_gate_b, w_up_b, w_down_b, l)

        kv_p = jnp.stack([kv_all[(b + 1) * SEQ - WINDOW:(b + 1) * SEQ] for b in range(BATCH)])
        kv_p = kv_p.reshape(BATCH, WINDOW, 2, N_KV_HEADS, HEAD_DIM)
        k_p.append(kv_p[:, :, 0])
        v_p.append(kv_p[:, :, 1])
        kv_s = kv_all[T_PROMPT:].reshape(DEC_BATCH, DEC_SEQ, 2, N_KV_HEADS, HEAD_DIM)
        k_s.append(jnp.concatenate([cache_k[l][:, DEC_SEQ:], kv_s[:, :, 0]], axis=1))
        v_s.append(jnp.concatenate([cache_v[l][:, DEC_SEQ:], kv_s[:, :, 1]], axis=1))
        hr_p.append(sfin_r.reshape(BATCH, G, P))
        hi_p.append(sfin_i.reshape(BATCH, G, P))
        hr_s.append(snew_r.reshape(DEC_BATCH, G, P))
        hi_s.append(snew_i.reshape(DEC_BATCH, G, P))

    g_fin = norm_final.reshape(1, D_MODEL)
    y_prompt = _final_norm(x_all, g_fin, T_PROMPT, 1024, 0).reshape(BATCH, SEQ, D_MODEL)
    y_sample = _final_norm(x_all, g_fin, T_SAMPLE, T_SAMPLE, T_PROMPT // T_SAMPLE).reshape(DEC_BATCH, DEC_SEQ, D_MODEL)
    st = jnp.stack
    return (y_prompt, y_sample, st(k_p), st(v_p), st(hr_p), st(hi_p), st(k_s), st(v_s), st(hr_s), st(hi_s))
```

```python
import functools
import math

import jax
import jax.numpy as jnp
from jax import lax
from jax.experimental import pallas as pl
from jax.experimental.pallas import tpu as pltpu

F32 = jnp.float32
BF16 = jnp.bfloat16

D_MODEL = 2048
BATCH = 2
SEQ = 4096
DEPTH = 4
DEC_BATCH = 32
DEC_SEQ = 4
ATTN_WIDTH = 1024
SSM_WIDTH = 1024
HEAD_DIM = 64
N_HEADS = 16
N_KV_HEADS = 2
GQA_GROUP = 8
KV_WIDTH = 128
WINDOW = 128
SSM_GROUP = 16
N_SSM_GROUPS = 64
SSM_STATE = 64
D_IN = ATTN_WIDTH + 2 * KV_WIDTH + SSM_WIDTH
D_FF = 5632
EPS = 1e-5

T_PROMPT = BATCH * SEQ
T_SAMPLE = DEC_BATCH * DEC_SEQ
T_ALL = T_PROMPT + T_SAMPLE
ROW_TILE = 1040
FF_TILE = 512
CHUNK_P = 16
CHUNK_S = DEC_SEQ
SLAB = 128
GROUPS_PER_SLAB = SLAB // SSM_GROUP
N_SLABS = SSM_WIDTH // SLAB
SLAB_STATE = GROUPS_PER_SLAB * SSM_STATE
MASK_NEG = -1e30
VMEM_LIMIT = 60 * 1024 * 1024

_TRANS_B = (((1,), (1,)), ((), ()))
_HI = lax.Precision.HIGHEST


def _cparams(*sem):
    return pltpu.CompilerParams(dimension_semantics=sem, vmem_limit_bytes=VMEM_LIMIT)


def _rms(x, g):
    ms = jnp.mean(x * x, axis=-1, keepdims=True)
    return x * lax.rsqrt(ms + EPS) * g


def _in_proj_kernel(x_ref, g_ref, w_ref, q_ref, kv_ref, u_ref):
    h = _rms(x_ref[...], g_ref[...]).astype(BF16)
    z = jnp.dot(h, w_ref[...], preferred_element_type=F32)
    q_ref[...] = (z[:, :ATTN_WIDTH] * (HEAD_DIM ** -0.5)).astype(BF16)
    kv_ref[...] = z[:, ATTN_WIDTH:ATTN_WIDTH + 2 * KV_WIDTH]
    u_ref[...] = z[:, ATTN_WIDTH + 2 * KV_WIDTH:]


def _in_proj_out(tile):
    row = lambda i: (i, 0)
    specs = [pl.BlockSpec((tile, ATTN_WIDTH), row), pl.BlockSpec((tile, 2 * KV_WIDTH), row),
             pl.BlockSpec((tile, SSM_WIDTH), row)]
    shapes = [jax.ShapeDtypeStruct((T_ALL, ATTN_WIDTH), BF16), jax.ShapeDtypeStruct((T_ALL, 2 * KV_WIDTH), F32),
              jax.ShapeDtypeStruct((T_ALL, SSM_WIDTH), F32)]
    return specs, shapes


def _in_proj(x_all, g, w_in_b, l):
    out_specs, out_shapes = _in_proj_out(ROW_TILE)
    return pl.pallas_call(
        _in_proj_kernel,
        grid=(T_ALL // ROW_TILE,),
        in_specs=[
            pl.BlockSpec((ROW_TILE, D_MODEL), lambda i: (i, 0)),
            pl.BlockSpec((None, 1, D_MODEL), lambda i: (l, 0, 0)),
            pl.BlockSpec((D_MODEL, D_IN), lambda i: (0, 0), pipeline_mode=pl.Buffered(1)),
        ],
        out_specs=out_specs,
        out_shape=out_shapes,
        compiler_params=_cparams("parallel"),
        name="in_proj",
    )(x_all, g, w_in_b)


FIRST_TILE = 640
_LAST_TILE = T_ALL // FIRST_TILE - 1
_PROMPT_ROWS_LAST_TILE = T_PROMPT - _LAST_TILE * FIRST_TILE


def _in_proj_first_kernel(xp_ref, xs_ref, g_ref, w_ref, q_ref, kv_ref, u_ref, x_ref):
    i = pl.program_id(0)

    @pl.when(i < _LAST_TILE)
    def _():
        x_ref[...] = xp_ref[...]

    @pl.when(i == _LAST_TILE)
    def _():
        x_ref[:_PROMPT_ROWS_LAST_TILE, :] = xp_ref[:_PROMPT_ROWS_LAST_TILE, :]
        x_ref[_PROMPT_ROWS_LAST_TILE:, :] = xs_ref[...]

    _in_proj_kernel(x_ref, g_ref, w_ref, q_ref, kv_ref, u_ref)


def _in_proj_first(x_prompt, x_sample, g, w_in_b):
    assert FIRST_TILE - _PROMPT_ROWS_LAST_TILE == T_SAMPLE
    out_specs, out_shapes = _in_proj_out(FIRST_TILE)
    return pl.pallas_call(
        _in_proj_first_kernel,
        grid=(T_ALL // FIRST_TILE,),
        in_specs=[
            pl.BlockSpec((FIRST_TILE, D_MODEL), lambda i: (i, 0)),
            pl.BlockSpec((T_SAMPLE, D_MODEL), lambda i: (0, 0)),
            pl.BlockSpec((None, 1, D_MODEL), lambda i: (0, 0, 0)),
            pl.BlockSpec((D_MODEL, D_IN), lambda i: (0, 0), pipeline_mode=pl.Buffered(1)),
        ],
        out_specs=out_specs + [pl.BlockSpec((FIRST_TILE, D_MODEL), lambda i: (i, 0))],
        out_shape=out_shapes + [jax.ShapeDtypeStruct((T_ALL, D_MODEL), F32)],
        compiler_params=_cparams("parallel"),
        name="in_proj_first",
    )(x_prompt, x_sample, g, w_in_b)


def _softmax_pv(s_parts, bias_parts, v_parts, sinks, rows):
    p_parts = [[] for _ in s_parts]
    dens = []
    for g in range(GQA_GROUP):
        sl = slice(g * rows, (g + 1) * rows)
        sk = sinks[g]
        sg = [s[sl] + b for s, b in zip(s_parts, bias_parts)]
        m = sk
        for s in sg:
            m = jnp.maximum(jnp.max(s, axis=-1, keepdims=True), m)
        den = jnp.exp(sk - m)
        for j, s in enumerate(sg):
            p = jnp.exp(s - m)
            den = den + jnp.sum(p, axis=-1, keepdims=True)
            p_parts[j].append(p.astype(BF16))
        dens.append(den)
    o = None
    for j, v in enumerate(v_parts):
        pj = jnp.concatenate(p_parts[j], axis=0)
        oj = jnp.dot(pj, v, preferred_element_type=F32)
        o = oj if o is None else o + oj
    return [o[g * rows:(g + 1) * rows] / dens[g] for g in range(GQA_GROUP)]


def _head_cols(h):
    return slice(h * HEAD_DIM, (h + 1) * HEAD_DIM)


def _stack_heads(q, hk):
    return jnp.concatenate([q[:, _head_cols(hk * GQA_GROUP + g)] for g in range(GQA_GROUP)], axis=0)


def _attn_prompt_block(n, sinks, q, kvc, kvp, gain):
    row = lax.broadcasted_iota(jnp.int32, (WINDOW, 2 * WINDOW), 0)
    col = lax.broadcasted_iota(jnp.int32, (WINDOW, 2 * WINDOW), 1)
    ok = (col > row) & (col <= row + WINDOW) & ((col >= WINDOW) | (n > 0))
    bias = jnp.where(ok, 0.0, MASK_NEG).astype(F32)
    outs = []
    for hk in range(N_KV_HEADS):
        ks, vs = _head_cols(hk), _head_cols(N_KV_HEADS + hk)
        k = jnp.concatenate([kvp[:, ks], kvc[:, ks]], axis=0).astype(BF16)
        v = jnp.concatenate([kvp[:, vs], kvc[:, vs]], axis=0).astype(BF16)
        for g in range(GQA_GROUP):
            h = hk * GQA_GROUP + g
            s = lax.dot_general(q[:, _head_cols(h)], k, _TRANS_B, preferred_element_type=F32) + bias
            m = jnp.maximum(jnp.max(s, axis=-1, keepdims=True), sinks[h])
            p = jnp.exp(s - m)
            den = jnp.sum(p, axis=-1, keepdims=True) + jnp.exp(sinks[h] - m)
            outs.append(jnp.dot(p.astype(BF16), v, preferred_element_type=F32) / den)
    return _rms(jnp.concatenate(outs, axis=1), gain).astype(BF16)


_SEQ_PER_CHUNK = 4
_ROWS_S = _SEQ_PER_CHUNK * DEC_SEQ


def _attn_sample_chunk(sinks, q, kv, kc, vc, gain):
    r = lax.broadcasted_iota(jnp.int32, (_ROWS_S, _SEQ_PER_CHUNK * WINDOW), 0)
    c = lax.broadcasted_iota(jnp.int32, (_ROWS_S, _SEQ_PER_CHUNK * WINDOW), 1)
    ok_c = (c // WINDOW == r // DEC_SEQ) & (c % WINDOW > r % DEC_SEQ)
    bias_c = jnp.where(ok_c, 0.0, MASK_NEG).astype(F32)
    r2 = lax.broadcasted_iota(jnp.int32, (_ROWS_S, _ROWS_S), 0)
    c2 = lax.broadcasted_iota(jnp.int32, (_ROWS_S, _ROWS_S), 1)
    ok_n = (c2 // DEC_SEQ == r2 // DEC_SEQ) & (c2 % DEC_SEQ <= r2 % DEC_SEQ)
    bias_n = jnp.where(ok_n, 0.0, MASK_NEG).astype(F32)
    outs = []
    for hk in range(N_KV_HEADS):
        ks, vs = _head_cols(hk), _head_cols(N_KV_HEADS + hk)
        k_c, v_c = kc[:, ks].astype(BF16), vc[:, ks].astype(BF16)
        k_n, v_n = kv[:, ks].astype(BF16), kv[:, vs].astype(BF16)
        qs = _stack_heads(q, hk)
        s_c = lax.dot_general(qs, k_c, _TRANS_B, preferred_element_type=F32)
        s_n = lax.dot_general(qs, k_n, _TRANS_B, preferred_element_type=F32)
        outs += _softmax_pv([s_c, s_n], [bias_c, bias_n], [v_c, v_n],
                            sinks[hk * GQA_GROUP:(hk + 1) * GQA_GROUP], _ROWS_S)
    return _rms(jnp.concatenate(outs, axis=1), gain).astype(BF16)


_N_PROMPT_BLOCKS = T_PROMPT // WINDOW


def _attn_kernel(n_cast, sink_ref, q_ref, kvc_ref, kvp_ref, kc_ref, vc_ref, g_ref, *rest):
    o_ref = rest[n_cast]
    for src_ref, dst_ref in zip(rest[:n_cast], rest[n_cast + 1:]):
        dst_ref[...] = src_ref[...].astype(BF16)
    i = pl.program_id(0)
    sinks = [sink_ref[h] for h in range(N_HEADS)]
    gain = g_ref[...]

    @pl.when(i < _N_PROMPT_BLOCKS)
    def _():
        o_ref[...] = _attn_prompt_block(i % (SEQ // WINDOW), sinks, q_ref[...], kvc_ref[...], kvp_ref[...], gain)

    @pl.when(i == _N_PROMPT_BLOCKS)
    def _():
        def chunk(c, carry):
            rows = pl.ds(pl.multiple_of(c * _ROWS_S, _ROWS_S), _ROWS_S)
            seqs = pl.ds(c * _SEQ_PER_CHUNK, _SEQ_PER_CHUNK)
            kc = kc_ref[seqs].reshape(_SEQ_PER_CHUNK * WINDOW, KV_WIDTH)
            vc = vc_ref[seqs].reshape(_SEQ_PER_CHUNK * WINDOW, KV_WIDTH)
            o_ref[rows, :] = _attn_sample_chunk(sinks, q_ref[rows, :], kvc_ref[rows, :], kc, vc, gain)
            return carry
        lax.fori_loop(0, DEC_BATCH // _SEQ_PER_CHUNK, chunk, 0)


def _attention(sink, q_all, kv_all, kc, vc, g, l, casts=()):
    nb = SEQ // WINDOW
    cache = pl.BlockSpec((None, DEC_BATCH, WINDOW, KV_WIDTH), lambda i: (l, 0, 0, 0), pipeline_mode=pl.Buffered(1))
    in_specs = [
        pl.BlockSpec(memory_space=pltpu.SMEM),
        pl.BlockSpec((WINDOW, ATTN_WIDTH), lambda i: (i, 0)),
        pl.BlockSpec((WINDOW, 2 * KV_WIDTH), lambda i: (i, 0)),
        pl.BlockSpec((WINDOW, 2 * KV_WIDTH), lambda i: (jnp.where(i % nb == 0, i, i - 1), 0)),
        cache,
        cache,
        pl.BlockSpec((None, 1, ATTN_WIDTH), lambda i: (l, 0, 0)),
    ]
    out_specs = [pl.BlockSpec((WINDOW, ATTN_WIDTH), lambda i: (i, 0))]
    out_shape = [jax.ShapeDtypeStruct((T_ALL, ATTN_WIDTH), BF16)]
    for w, layer, rows in casts:
        _, n_rows, n_cols = w.shape
        n_blocks = n_rows // rows
        assert n_blocks * rows == n_rows and _N_PROMPT_BLOCKS % n_blocks == 0
        steps = _N_PROMPT_BLOCKS // n_blocks
        blk = lambda i, steps=steps, n_blocks=n_blocks: jnp.minimum(i // steps, n_blocks - 1)
        in_specs.append(pl.BlockSpec((None, rows, n_cols), lambda i, blk=blk, layer=layer: (layer, blk(i), 0)))
        out_specs.append(pl.BlockSpec((rows, n_cols), lambda i, blk=blk: (blk(i), 0)))
        out_shape.append(jax.ShapeDtypeStruct((n_rows, n_cols), BF16))
    return pl.pallas_call(
        functools.partial(_attn_kernel, len(casts)),
        grid=(_N_PROMPT_BLOCKS + 1,),
        in_specs=in_specs,
        out_specs=out_specs,
        out_shape=out_shape,
        compiler_params=_cparams("arbitrary"),
        name="attention",
    )(sink, q_all, kv_all, kv_all, kc, vc, g, *[w for w, _, _ in casts])


def _expand_groups(m):
    m2 = m.reshape(SLAB, SSM_STATE)
    t = jnp.concatenate([m2] * GROUPS_PER_SLAB, axis=1)
    r = lax.broadcasted_iota(jnp.int32, (SLAB, SLAB_STATE), 0) // SSM_GROUP
    c = lax.broadcasted_iota(jnp.int32, (SLAB, SLAB_STATE), 1) // SSM_STATE
    return jnp.where(r == c, t, 0.0)


def _fold_groups(x):
    t = [x[:, i * SLAB:(i + 1) * SLAB] for i in range(2 * SLAB_STATE // SLAB)]
    return jnp.concatenate([t[0] + t[1] + t[2] + t[3], t[4] + t[5] + t[6] + t[7]], axis=1)


def _ssm_tables(lam_ref, logdt_ref, bre_ref, bim_ref, cre_ref, cim_ref, max_chunk):
    H = SLAB_STATE
    dt = jnp.exp(logdt_ref[...])
    lam = lam_ref[...]
    lr, li = lam[0:1], lam[1:2]
    n_tau = ((max_chunk + 1 + 7) // 8) * 8
    tau = lax.broadcasted_iota(jnp.int32, (n_tau, H), 0).astype(F32)
    mag = jnp.exp(tau * (lr * dt))
    ang = tau * (li * dt)
    pw_r, pw_i = mag * jnp.cos(ang), mag * jnp.sin(ang)
    abr, abi = pw_r[1:2], pw_i[1:2]
    nr = abr - 1.0
    den = lr * lr + li * li
    cr = (nr * lr + abi * li) / den
    ci = (abi * lr - nr * li) / den
    xb_r, xb_i = _expand_groups(bre_ref[...]), _expand_groups(bim_ref[...])
    xc_r, xc_i = _expand_groups(cre_ref[...]), _expand_groups(cim_ref[...])
    return pw_r, pw_i, cr, ci, xb_r, xb_i, xc_r, xc_i


def _ssm_operators(L, tables, tt_ref, bst_ref, cst_ref):
    pw_r, pw_i, cr, ci, xb_r, xb_i, xc_r, xc_i = tables
    ccm = _fold_groups(jnp.concatenate([xc_r, -xc_i], axis=1))
    same_group = (lax.broadcasted_iota(jnp.int32, (SLAB, SLAB), 0) // SSM_GROUP
                  == lax.broadcasted_iota(jnp.int32, (SLAB, SLAB), 1) // SSM_GROUP)

    k_lag = [None] * L
    for s in range(L):
        e = L - 1 - s
        w_r = cr * pw_r[e:e + 1] - ci * pw_i[e:e + 1]
        w_i = cr * pw_i[e:e + 1] + ci * pw_r[e:e + 1]
        slab = jnp.concatenate([xb_r * w_r - xb_i * w_i, xb_r * w_i + xb_i * w_r], axis=1)
        bst_ref[s * SLAB:(s + 1) * SLAB, :] = slab.astype(BF16)
        k_all = lax.dot_general(_fold_groups(slab), ccm, _TRANS_B, preferred_element_type=F32, precision=_HI)
        k_lag[e] = jnp.where(same_group, k_all, 0.0)
    for s in range(L):
        for t_lo in range(2):
            lag = L - 2 + t_lo - s
            blk = k_lag[lag] if lag >= 0 else jnp.zeros((SLAB, SLAB), F32)
            tt_ref[s * SLAB:(s + 1) * SLAB, t_lo * SLAB:(t_lo + 1) * SLAB] = blk.astype(BF16)
    for t in range(L):
        p_r, p_i = pw_r[t + 1:t + 2], pw_i[t + 1:t + 2]
        g_t = jnp.concatenate([xc_r * p_r - xc_i * p_i, -(xc_r * p_i + xc_i * p_r)], axis=1)
        cst_ref[:, t * SLAB:(t + 1) * SLAB] = g_t.T.astype(BF16)
    return pw_r[L:L + 1], pw_i[L:L + 1]


_SCAN_ROWS = 8


def _scan_chunk_states(s_r, s_i, a_r, a_i, n_per_seq):
    n, H = s_r.shape
    j8 = lax.broadcasted_iota(jnp.int32, (n, H), 0) % _SCAN_ROWS
    p_r, p_i = a_r, a_i
    d = 1
    while d < _SCAN_ROWS:
        keep = j8 >= d
        sh_r = jnp.where(keep, pltpu.roll(s_r, d, axis=0), 0.0)
        sh_i = jnp.where(keep, pltpu.roll(s_i, d, axis=0), 0.0)
        s_r, s_i = s_r + sh_r * p_r - sh_i * p_i, s_i + sh_r * p_i + sh_i * p_r
        p_r, p_i = p_r * p_r - p_i * p_i, 2.0 * p_r * p_i
        d *= 2
    rows_r, rows_i = [a_r], [a_i]
    for _ in range(_SCAN_ROWS - 1):
        q_r, q_i = rows_r[-1], rows_i[-1]
        rows_r.append(q_r * a_r - q_i * a_i)
        rows_i.append(q_r * a_i + q_i * a_r)
    t_r, t_i = jnp.concatenate(rows_r, axis=0), jnp.concatenate(rows_i, axis=0)
    out_r, out_i = [], []
    for g in range(n // _SCAN_ROWS):
        b_r = s_r[g * _SCAN_ROWS:(g + 1) * _SCAN_ROWS]
        b_i = s_i[g * _SCAN_ROWS:(g + 1) * _SCAN_ROWS]
        if (g * _SCAN_ROWS) % n_per_seq != 0:
            c_r, c_i = out_r[-1][_SCAN_ROWS - 1:], out_i[-1][_SCAN_ROWS - 1:]
            b_r, b_i = b_r + t_r * c_r - t_i * c_i, b_i + t_r * c_i + t_i * c_r
        out_r.append(b_r)
        out_i.append(b_i)
    return jnp.concatenate(out_r, axis=0), jnp.concatenate(out_i, axis=0)


def _ssm_rows(u_ref, y_ref, row0, n_rows, L, ops, n_per_seq=None, h0=None):
    tt_ref, bst_ref, cst_ref, (a_r, a_i) = ops
    H = SLAB_STATE
    nj = n_rows // L
    lhs = jnp.concatenate([u_ref[pl.ds(row0 + s, nj, stride=L), :].astype(BF16) for s in range(L)], axis=1)
    e = jnp.dot(lhs, bst_ref[...], preferred_element_type=F32)
    s_r, s_i = e[:, :H], e[:, H:]
    if h0 is not None:
        in_r, in_i = h0
        s_r = s_r + in_r * a_r - in_i * a_i
        s_i = s_i + in_r * a_i + in_i * a_r
    else:
        s_r, s_i = _scan_chunk_states(s_r, s_i, a_r, a_i, n_per_seq)
        jj = lax.broadcasted_iota(jnp.int32, (nj, H), 0) % n_per_seq
        in_r = jnp.where(jj >= 1, pltpu.roll(s_r, 1, axis=0), 0.0)
        in_i = jnp.where(jj >= 1, pltpu.roll(s_i, 1, axis=0), 0.0)
    y_st = jnp.dot(jnp.concatenate([in_r, in_i], axis=1).astype(BF16), cst_ref[...], preferred_element_type=F32)
    for tp in range(L // 2):
        k = 2 * SLAB * (tp + 1)
        y = jnp.dot(lhs[:, :k], tt_ref[2 * SLAB * (L // 2 - 1 - tp):, :], preferred_element_type=F32)
        y = y + y_st[:, 2 * SLAB * tp:2 * SLAB * (tp + 1)]
        y_ref[pl.ds(row0 + 2 * tp, nj, stride=L), :] = y[:, :SLAB]
        y_ref[pl.ds(row0 + 2 * tp + 1, nj, stride=L), :] = y[:, SLAB:]
    return s_r, s_i


def _ssm_kernel(u_ref, h0r_ref, h0i_ref, lam_ref, logdt_ref, bre_ref, bim_ref, cre_ref, cim_ref,
                y_ref, pr_ref, pi_ref, sr_ref, si_ref,
                tt_p, bst_p, cst_p, tt_s, bst_s, cst_s):
    tables = _ssm_tables(lam_ref, logdt_ref, bre_ref, bim_ref, cre_ref, cim_ref, max(CHUNK_P, CHUNK_S))
    al_p = _ssm_operators(CHUNK_P, tables, tt_p, bst_p, cst_p)
    al_s = _ssm_operators(CHUNK_S, tables, tt_s, bst_s, cst_s)
    n_per_seq = SEQ // CHUNK_P
    s_r, s_i = _ssm_rows(u_ref, y_ref, 0, T_PROMPT, CHUNK_P, (tt_p, bst_p, cst_p, al_p), n_per_seq=n_per_seq)
    for b in range(BATCH):
        last = (b + 1) * n_per_seq - 1
        pr_ref[b:b + 1, :] = s_r[last:last + 1]
        pi_ref[b:b + 1, :] = s_i[last:last + 1]
    n_r, n_i = _ssm_rows(u_ref, y_ref, T_PROMPT, T_SAMPLE, CHUNK_S, (tt_s, bst_s, cst_s, al_s),
                         h0=(h0r_ref[...], h0i_ref[...]))
    sr_ref[...] = n_r
    si_ref[...] = n_i


def _ssm_operator_scratch(L):
    H = SLAB_STATE
    return [pltpu.VMEM((L * SLAB, 2 * SLAB), BF16), pltpu.VMEM((L * SLAB, 2 * H), BF16),
            pltpu.VMEM((2 * H, L * SLAB), BF16)]


def _ssm(u_all, h0_r, h0_i, lam, logdt, bt_re, bt_im, c_re, c_im):
    H = SLAB_STATE
    slab = lambda rows: pl.BlockSpec((rows, SLAB), lambda q: (0, q))
    state = lambda rows: pl.BlockSpec((rows, H), lambda q: (0, q))
    q3 = lambda *blk: pl.BlockSpec((None,) + blk, lambda q: (q, 0, 0))
    grp = pl.BlockSpec((GROUPS_PER_SLAB, SSM_GROUP, SSM_STATE), lambda q: (q, 0, 0))
    return pl.pallas_call(
        _ssm_kernel,
        grid=(N_SLABS,),
        in_specs=[slab(T_ALL), state(DEC_BATCH), state(DEC_BATCH), q3(2, H), q3(1, H), grp, grp, grp, grp],
        out_specs=[slab(T_ALL), state(BATCH), state(BATCH), state(DEC_BATCH), state(DEC_BATCH)],
        out_shape=[
            jax.ShapeDtypeStruct((T_ALL, SSM_WIDTH), F32),
            jax.ShapeDtypeStruct((BATCH, N_SLABS * H), F32),
            jax.ShapeDtypeStruct((BATCH, N_SLABS * H), F32),
            jax.ShapeDtypeStruct((DEC_BATCH, N_SLABS * H), F32),
            jax.ShapeDtypeStruct((DEC_BATCH, N_SLABS * H), F32),
        ],
        scratch_shapes=_ssm_operator_scratch(CHUNK_P) + _ssm_operator_scratch(CHUNK_S),
        compiler_params=_cparams("parallel"),
        name="ssm",
    )(u_all, h0_r, h0_i, lam, logdt, bt_re, bt_im, c_re, c_im)


def _glu_kernel(y_ref, u_ref, d_ref, w_ref, g_ref, o_ref):
    y = y_ref[...] + d_ref[...] * u_ref[...]
    z = 0.5 * y * (1.0 + lax.erf(y * (1.0 / math.sqrt(2.0))))
    gate = jnp.dot(z.astype(BF16), w_ref[...], preferred_element_type=F32)
    out = z * jax.nn.sigmoid(gate)
    o_ref[...] = _rms(out, g_ref[...]).astype(BF16)


def _glu(y_all, u_all, d, w_glu_b, g, l):
    row = lambda i: (i, 0)
    par = lambda i: (l, 0, 0)
    return pl.pallas_call(
        _glu_kernel,
        grid=(T_ALL // ROW_TILE,),
        in_specs=[
            pl.BlockSpec((ROW_TILE, SSM_WIDTH), row),
            pl.BlockSpec((ROW_TILE, SSM_WIDTH), row),
            pl.BlockSpec((None, 1, SSM_WIDTH), par),
            pl.BlockSpec((SSM_WIDTH, SSM_WIDTH), lambda i: (0, 0), pipeline_mode=pl.Buffered(1)),
            pl.BlockSpec((None, 1, SSM_WIDTH), par),
        ],
        out_specs=pl.BlockSpec((ROW_TILE, SSM_WIDTH), row),
        out_shape=jax.ShapeDtypeStruct((T_ALL, SSM_WIDTH), BF16),
        compiler_params=_cparams("parallel"),
        name="glu",
    )(y_all, u_all, d, w_glu_b, g)


def _out_proj_kernel(a_ref, s_ref, w_ref, x_ref, o_ref):
    mixed = jnp.dot(jnp.concatenate([a_ref[...], s_ref[...]], axis=1), w_ref[...], preferred_element_type=F32)
    o_ref[...] = x_ref[...] + mixed


def _out_proj(a_n, s_n, w_out_b, x_all, l):
    row = lambda i: (i, 0)
    return pl.pallas_call(
        _out_proj_kernel,
        grid=(T_ALL // ROW_TILE,),
        in_specs=[
            pl.BlockSpec((ROW_TILE, ATTN_WIDTH), row),
            pl.BlockSpec((ROW_TILE, SSM_WIDTH), row),
            pl.BlockSpec((D_MODEL, D_MODEL), lambda i: (0, 0), pipeline_mode=pl.Buffered(1)),
            pl.BlockSpec((ROW_TILE, D_MODEL), row),
        ],
        out_specs=pl.BlockSpec((ROW_TILE, D_MODEL), row),
        out_shape=jax.ShapeDtypeStruct((T_ALL, D_MODEL), F32),
        compiler_params=_cparams("parallel"),
        name="out_proj",
    )(a_n, s_n, w_out_b, x_all)


def _ffn_body(x_ref, g_ref, wg_ref, wu_ref, wd_ref, o_ref, h_ref):
    @pl.when(pl.program_id(1) == 0)
    def _():
        x = x_ref[...]
        h_ref[...] = _rms(x, g_ref[...]).astype(BF16)
        o_ref[...] = x

    h = h_ref[...]
    gate = jnp.dot(h, wg_ref[...], preferred_element_type=F32)
    up = jnp.dot(h, wu_ref[...], preferred_element_type=F32)
    act = (gate * jax.nn.sigmoid(gate) * up).astype(BF16)
    o_ref[...] += jnp.dot(act, wd_ref[...], preferred_element_type=F32)


def _ffn_kernel(x_ref, g_ref, wg_ref, wu_ref, wd_ref, o_ref, h_ref):
    _ffn_body(x_ref, g_ref, wg_ref, wu_ref, wd_ref, o_ref, h_ref)


def _ffn_cast_kernel(x_ref, g_ref, wg_ref, wu_ref, wd_ref, ng_ref, nu_ref, nd_ref,
                     o_ref, ngb_ref, nub_ref, ndb_ref, h_ref):
    _ffn_body(x_ref, g_ref, wg_ref, wu_ref, wd_ref, o_ref, h_ref)
    ngb_ref[...] = ng_ref[...].astype(BF16)
    nub_ref[...] = nu_ref[...].astype(BF16)
    ndb_ref[...] = nd_ref[...].astype(BF16)


_N_ROW_TILES = T_ALL // ROW_TILE
_N_FF_TILES = D_FF // FF_TILE
_CAST_ROWS = D_MODEL // _N_ROW_TILES


def _ffn(x_all, g, w_gate_b, w_up_b, w_down_b, l, w_next=None):
    in_specs = [
        pl.BlockSpec((ROW_TILE, D_MODEL), lambda i, f: (i, 0)),
        pl.BlockSpec((None, 1, D_MODEL), lambda i, f: (l, 0, 0)),
        pl.BlockSpec((D_MODEL, FF_TILE), lambda i, f: (0, f)),
        pl.BlockSpec((D_MODEL, FF_TILE), lambda i, f: (0, f)),
        pl.BlockSpec((FF_TILE, D_MODEL), lambda i, f: (f, 0)),
    ]
    out_specs = [pl.BlockSpec((ROW_TILE, D_MODEL), lambda i, f: (i, 0))]
    out_shape = [jax.ShapeDtypeStruct((T_ALL, D_MODEL), F32)]
    args = [x_all, g, w_gate_b, w_up_b, w_down_b]
    if w_next is not None:
        assert D_MODEL % _N_ROW_TILES == 0
        in_specs += [
            pl.BlockSpec((None, _CAST_ROWS, FF_TILE), lambda i, f: (l + 1, i, f)),
            pl.BlockSpec((None, _CAST_ROWS, FF_TILE), lambda i, f: (l + 1, i, f)),
            pl.BlockSpec((None, FF_TILE, _CAST_ROWS), lambda i, f: (l + 1, f, i)),
        ]
        out_specs += [
            pl.BlockSpec((_CAST_ROWS, FF_TILE), lambda i, f: (i, f)),
            pl.BlockSpec((_CAST_ROWS, FF_TILE), lambda i, f: (i, f)),
            pl.BlockSpec((FF_TILE, _CAST_ROWS), lambda i, f: (f, i)),
        ]
        out_shape += [jax.ShapeDtypeStruct((D_MODEL, D_FF), BF16), jax.ShapeDtypeStruct((D_MODEL, D_FF), BF16),
                      jax.ShapeDtypeStruct((D_FF, D_MODEL), BF16)]
        args += list(w_next)
    return pl.pallas_call(
        _ffn_kernel if w_next is None else _ffn_cast_kernel,
        grid=(_N_ROW_TILES, _N_FF_TILES),
        in_specs=in_specs,
        out_specs=out_specs,
        out_shape=out_shape,
        scratch_shapes=[pltpu.VMEM((ROW_TILE, D_MODEL), BF16)],
        compiler_params=_cparams("parallel", "arbitrary"),
        name="ffn",
    )(*args)


def _norm_kernel(x_ref, g_ref, o_ref):
    o_ref[...] = _rms(x_ref[...], g_ref[...])


def _final_norm(x_all, g, rows, tile, first_block):
    return pl.pallas_call(
        _norm_kernel,
        grid=(rows // tile,),
        in_specs=[
            pl.BlockSpec((tile, D_MODEL), lambda i: (first_block + i, 0)),
            pl.BlockSpec((1, D_MODEL), lambda i: (0, 0)),
        ],
        out_specs=pl.BlockSpec((tile, D_MODEL), lambda i: (i, 0)),
        out_shape=jax.ShapeDtypeStruct((rows, D_MODEL), F32),
        compiler_params=_cparams("parallel"),
        name="final_norm",
    )(x_all, g)


def kernel(x_prompt, x_sample, cache_k, cache_v, state_ssm_re, state_ssm_im, norm_mix, w_in, attn_sink, ssm_a_re,
           ssm_a_im, ssm_log_dt, ssm_b_re, ssm_b_im, ssm_c_re, ssm_c_im, ssm_d, w_glu, norm_attn_out, norm_ssm_out,
           w_out, norm_ffn, w_gate, w_up, w_down, norm_final):
    G, P, H = N_SSM_GROUPS, SSM_STATE, SLAB_STATE
    w_in_b, w_out_b, w_glu_b = w_in[0].astype(BF16), w_out[0].astype(BF16), w_glu[0].astype(BF16)
    row3 = lambda p: p.reshape(DEPTH, 1, -1)
    g_mix, g_attn, g_ssm, g_ffn, d3 = row3(norm_mix), row3(norm_attn_out), row3(norm_ssm_out), row3(norm_ffn), row3(ssm_d)
    cache_k2 = cache_k.reshape(DEPTH, DEC_BATCH, WINDOW, KV_WIDTH)
    cache_v2 = cache_v.reshape(DEPTH, DEC_BATCH, WINDOW, KV_WIDTH)

    k_p, v_p, hr_p, hi_p, k_s, v_s, hr_s, hi_s = [], [], [], [], [], [], [], []
    for l in range(DEPTH):
        if l == 0:
            q_all, kv_all, u_all, x_all = _in_proj_first(x_prompt.reshape(T_PROMPT, D_MODEL),
                                                         x_sample.reshape(T_SAMPLE, D_MODEL), g_mix, w_in_b)
        else:
            q_all, kv_all, u_all = _in_proj(x_all, g_mix, w_in_b, l)

        casts = []
        if l == 0:
            casts += [(w_gate, 0, D_MODEL // 64), (w_up, 0, D_MODEL // 64), (w_down, 0, D_FF // 32)]
        if l + 1 < DEPTH:
            casts += [(w_in, l + 1, D_MODEL // 64), (w_out, l + 1, D_MODEL // 64), (w_glu, l + 1, SSM_WIDTH // 64)]
        a_n, *cast_out = _attention(attn_sink[l], q_all, kv_all, cache_k2, cache_v2, g_attn, l, casts)
        if l == 0:
            ffn_w, cast_out = cast_out[:3], cast_out[3:]
        w_glu_cur, w_out_cur = w_glu_b, w_out_b
        if l + 1 < DEPTH:
            w_in_b, w_out_b, w_glu_b = cast_out

        lam = jnp.stack([ssm_a_re[l].reshape(N_SLABS, H), ssm_a_im[l].reshape(N_SLABS, H)], axis=1)
        logdt = jnp.repeat(ssm_log_dt[l], P).reshape(N_SLABS, 1, H)
        y_all, sfin_r, sfin_i, snew_r, snew_i = _ssm(
            u_all, state_ssm_re[l].reshape(DEC_BATCH, G * P), state_ssm_im[l].reshape(DEC_BATCH, G * P),
            lam, logdt, ssm_b_re[l].transpose(0, 2, 1), ssm_b_im[l].transpose(0, 2, 1), ssm_c_re[l], ssm_c_im[l])
        s_n = _glu(y_all, u_all, d3, w_glu_cur, g_ssm, l)

        x_all = _out_proj(a_n, s_n, w_out_cur, x_all, l)
        if l + 1 < DEPTH:
            x_all, *ffn_w = _ffn(x_all, g_ffn, *ffn_w, l, w_next=(w_gate, w_up, w_down))
        else:
            (x_all,) = _ffn(x_all, g_ffn, *ffn_w, l)

        kv_p = jnp.stack([kv_all[(b + 1) * SEQ - WINDOW:(b + 1) * SEQ] for b in range(BATCH)])
        kv_p = kv_p.reshape(BATCH, WINDOW, 2, N_KV_HEADS, HEAD_DIM)
        k_p.append(kv_p[:, :, 0])
        v_p.append(kv_p[:, :, 1])
        kv_s = kv_all[T_PROMPT:].reshape(DEC_BATCH, DEC_SEQ, 2, N_KV_HEADS, HEAD_DIM)
        k_s.append(jnp.concatenate([cache_k[l][:, DEC_SEQ:], kv_s[:, :, 0]], axis=1))
        v_s.append(jnp.concatenate([cache_v[l][:, DEC_SEQ:], kv_s[:, :, 1]], axis=1))
        hr_p.append(sfin_r.reshape(BATCH, G, P))
        hi_p.append(sfin_i.reshape(BATCH, G, P))
        hr_s.append(snew_r.reshape(DEC_BATCH, G, P))
        hi_s.append(snew_i.reshape(DEC_BATCH, G, P))

    g_fin = norm_final.reshape(1, D_MODEL)
    y_prompt = _final_norm(x_all, g_fin, T_PROMPT, 1024, 0).reshape(BATCH, SEQ, D_MODEL)
    y_sample = _final_norm(x_all, g_fin, T_SAMPLE, T_SAMPLE, T_PROMPT // T_SAMPLE).reshape(DEC_BATCH, DEC_SEQ, D_MODEL)
    st = jnp.stack
    return (y_prompt, y_sample, st(k_p), st(v_p), st(hr_p), st(hi_p), st(k_s), st(v_s), st(hr_s), st(hi_s))
```

```python
import functools
import math

import jax
import jax.numpy as jnp
from jax import lax
from jax.experimental import pallas as pl
from jax.experimental.pallas import tpu as pltpu

F32 = jnp.float32
BF16 = jnp.bfloat16

D_MODEL = 2048
BATCH = 2
SEQ = 4096
DEPTH = 4
DEC_BATCH = 32
DEC_SEQ = 4
ATTN_WIDTH = 1024
SSM_WIDTH = 1024
HEAD_DIM = 64
N_HEADS = 16
N_KV_HEADS = 2
GQA_GROUP = 8
KV_WIDTH = 128
WINDOW = 128
SSM_GROUP = 16
N_SSM_GROUPS = 64
SSM_STATE = 64
D_IN = ATTN_WIDTH + 2 * KV_WIDTH + SSM_WIDTH
D_FF = 5632
EPS = 1e-5

T_PROMPT = BATCH * SEQ
T_SAMPLE = DEC_BATCH * DEC_SEQ
T_ALL = T_PROMPT + T_SAMPLE
ROW_TILE = 1040
FF_TILE = 512
CHUNK_P = 16
CHUNK_S = DEC_SEQ
SLAB = 128
GROUPS_PER_SLAB = SLAB // SSM_GROUP
N_SLABS = SSM_WIDTH // SLAB
SLAB_STATE = GROUPS_PER_SLAB * SSM_STATE
MASK_NEG = -1e30
VMEM_LIMIT = 60 * 1024 * 1024

_TRANS_B = (((1,), (1,)), ((), ()))
_HI = lax.Precision.HIGHEST


def _cparams(*sem):
    return pltpu.CompilerParams(dimension_semantics=sem, vmem_limit_bytes=VMEM_LIMIT)


def _rms(x, g):
    ms = jnp.mean(x * x, axis=-1, keepdims=True)
    return x * lax.rsqrt(ms + EPS) * g


def _in_proj_kernel(x_ref, g_ref, w_ref, q_ref, kv_ref, u_ref):
    h = _rms(x_ref[...], g_ref[...]).astype(BF16)
    z = jnp.dot(h, w_ref[...], preferred_element_type=F32)
    q_ref[...] = (z[:, :ATTN_WIDTH] * (HEAD_DIM ** -0.5)).astype(BF16)
    kv_ref[...] = z[:, ATTN_WIDTH:ATTN_WIDTH + 2 * KV_WIDTH]
    u_ref[...] = z[:, ATTN_WIDTH + 2 * KV_WIDTH:]


def _in_proj_out(tile):
    row = lambda i: (i, 0)
    specs = [pl.BlockSpec((tile, ATTN_WIDTH), row), pl.BlockSpec((tile, 2 * KV_WIDTH), row),
             pl.BlockSpec((tile, SSM_WIDTH), row)]
    shapes = [jax.ShapeDtypeStruct((T_ALL, ATTN_WIDTH), BF16), jax.ShapeDtypeStruct((T_ALL, 2 * KV_WIDTH), F32),
              jax.ShapeDtypeStruct((T_ALL, SSM_WIDTH), F32)]
    return specs, shapes


def _in_proj(x_all, g, w_in_b, l):
    out_specs, out_shapes = _in_proj_out(ROW_TILE)
    return pl.pallas_call(
        _in_proj_kernel,
        grid=(T_ALL // ROW_TILE,),
        in_specs=[
            pl.BlockSpec((ROW_TILE, D_MODEL), lambda i: (i, 0)),
            pl.BlockSpec((None, 1, D_MODEL), lambda i: (l, 0, 0)),
            pl.BlockSpec((D_MODEL, D_IN), lambda i: (0, 0), pipeline_mode=pl.Buffered(1)),
        ],
        out_specs=out_specs,
        out_shape=out_shapes,
        compiler_params=_cparams("parallel"),
        name="in_proj",
    )(x_all, g, w_in_b)


FIRST_TILE = 640
_LAST_TILE = T_ALL // FIRST_TILE - 1
_PROMPT_ROWS_LAST_TILE = T_PROMPT - _LAST_TILE * FIRST_TILE


def _in_proj_first_kernel(xp_ref, xs_ref, g_ref, w_ref, q_ref, kv_ref, u_ref, x_ref):
    i = pl.program_id(0)

    @pl.when(i < _LAST_TILE)
    def _():
        x_ref[...] = xp_ref[...]

    @pl.when(i == _LAST_TILE)
    def _():
        x_ref[:_PROMPT_ROWS_LAST_TILE, :] = xp_ref[:_PROMPT_ROWS_LAST_TILE, :]
        x_ref[_PROMPT_ROWS_LAST_TILE:, :] = xs_ref[...]

    _in_proj_kernel(x_ref, g_ref, w_ref, q_ref, kv_ref, u_ref)


def _in_proj_first(x_prompt, x_sample, g, w_in_b):
    assert FIRST_TILE - _PROMPT_ROWS_LAST_TILE == T_SAMPLE
    out_specs, out_shapes = _in_proj_out(FIRST_TILE)
    return pl.pallas_call(
        _in_proj_first_kernel,
        grid=(T_ALL // FIRST_TILE,),
        in_specs=[
            pl.BlockSpec((FIRST_TILE, D_MODEL), lambda i: (i, 0)),
            pl.BlockSpec((T_SAMPLE, D_MODEL), lambda i: (0, 0)),
            pl.BlockSpec((None, 1, D_MODEL), lambda i: (0, 0, 0)),
            pl.BlockSpec((D_MODEL, D_IN), lambda i: (0, 0), pipeline_mode=pl.Buffered(1)),
        ],
        out_specs=out_specs + [pl.BlockSpec((FIRST_TILE, D_MODEL), lambda i: (i, 0))],
        out_shape=out_shapes + [jax.ShapeDtypeStruct((T_ALL, D_MODEL), F32)],
        compiler_params=_cparams("parallel"),
        name="in_proj_first",
    )(x_prompt, x_sample, g, w_in_b)


def _softmax_pv(s_parts, bias_parts, v_parts, sinks, rows):
    p_parts = [[] for _ in s_parts]
    dens = []
    for g in range(GQA_GROUP):
        sl = slice(g * rows, (g + 1) * rows)
        sk = sinks[g]
        sg = [s[sl] + b for s, b in zip(s_parts, bias_parts)]
        m = sk
        for s in sg:
            m = jnp.maximum(jnp.max(s, axis=-1, keepdims=True), m)
        den = jnp.exp(sk - m)
        for j, s in enumerate(sg):
            p = jnp.exp(s - m)
            den = den + jnp.sum(p, axis=-1, keepdims=True)
            p_parts[j].append(p.astype(BF16))
        dens.append(den)
    o = None
    for j, v in enumerate(v_parts):
        pj = jnp.concatenate(p_parts[j], axis=0)
        oj = jnp.dot(pj, v, preferred_element_type=F32)
        o = oj if o is None else o + oj
    return [o[g * rows:(g + 1) * rows] / dens[g] for g in range(GQA_GROUP)]


def _head_cols(h):
    return slice(h * HEAD_DIM, (h + 1) * HEAD_DIM)


def _stack_heads(q, hk):
    return jnp.concatenate([q[:, _head_cols(hk * GQA_GROUP + g)] for g in range(GQA_GROUP)], axis=0)


def _attn_prompt_block(n, sinks, q, kvc, kvp, gain):
    row = lax.broadcasted_iota(jnp.int32, (WINDOW, 2 * WINDOW), 0)
    col = lax.broadcasted_iota(jnp.int32, (WINDOW, 2 * WINDOW), 1)
    ok = (col > row) & (col <= row + WINDOW) & ((col >= WINDOW) | (n > 0))
    bias = jnp.where(ok, 0.0, MASK_NEG).astype(F32)
    outs = []
    for hk in range(N_KV_HEADS):
        ks, vs = _head_cols(hk), _head_cols(N_KV_HEADS + hk)
        k = jnp.concatenate([kvp[:, ks], kvc[:, ks]], axis=0).astype(BF16)
        v = jnp.concatenate([kvp[:, vs], kvc[:, vs]], axis=0).astype(BF16)
        for g in range(GQA_GROUP):
            h = hk * GQA_GROUP + g
            s = lax.dot_general(q[:, _head_cols(h)], k, _TRANS_B, preferred_element_type=F32) + bias
            m = jnp.maximum(jnp.max(s, axis=-1, keepdims=True), sinks[h])
            p = jnp.exp(s - m)
            den = jnp.sum(p, axis=-1, keepdims=True) + jnp.exp(sinks[h] - m)
            outs.append(jnp.dot(p.astype(BF16), v, preferred_element_type=F32) / den)
    return _rms(jnp.concatenate(outs, axis=1), gain).astype(BF16)


_SEQ_PER_CHUNK = 4
_ROWS_S = _SEQ_PER_CHUNK * DEC_SEQ


def _attn_sample_chunk(sinks, q, kv, kc, vc, gain):
    r = lax.broadcasted_iota(jnp.int32, (_ROWS_S, _SEQ_PER_CHUNK * WINDOW), 0)
    c = lax.broadcasted_iota(jnp.int32, (_ROWS_S, _SEQ_PER_CHUNK * WINDOW), 1)
    ok_c = (c // WINDOW == r // DEC_SEQ) & (c % WINDOW > r % DEC_SEQ)
    bias_c = jnp.where(ok_c, 0.0, MASK_NEG).astype(F32)
    r2 = lax.broadcasted_iota(jnp.int32, (_ROWS_S, _ROWS_S), 0)
    c2 = lax.broadcasted_iota(jnp.int32, (_ROWS_S, _ROWS_S), 1)
    ok_n = (c2 // DEC_SEQ == r2 // DEC_SEQ) & (c2 % DEC_SEQ <= r2 % DEC_SEQ)
    bias_n = jnp.where(ok_n, 0.0, MASK_NEG).astype(F32)
    outs = []
    for hk in range(N_KV_HEADS):
        ks, vs = _head_cols(hk), _head_cols(N_KV_HEADS + hk)
        k_c, v_c = kc[:, ks].astype(BF16), vc[:, ks].astype(BF16)
        k_n, v_n = kv[:, ks].astype(BF16), kv[:, vs].astype(BF16)
        qs = _stack_heads(q, hk)
        s_c = lax.dot_general(qs, k_c, _TRANS_B, preferred_element_type=F32)
        s_n = lax.dot_general(qs, k_n, _TRANS_B, preferred_element_type=F32)
        outs += _softmax_pv([s_c, s_n], [bias_c, bias_n], [v_c, v_n],
                            sinks[hk * GQA_GROUP:(hk + 1) * GQA_GROUP], _ROWS_S)
    return _rms(jnp.concatenate(outs, axis=1), gain).astype(BF16)


_N_PROMPT_BLOCKS = T_PROMPT // WINDOW


def _attn_kernel(n_cast, sink_ref, q_ref, kvc_ref, kvp_ref, kc_ref, vc_ref, g_ref, *rest):
    o_ref = rest[n_cast]
    for src_ref, dst_ref in zip(rest[:n_cast], rest[n_cast + 1:]):
        dst_ref[...] = src_ref[...].astype(BF16)
    i = pl.program_id(0)
    sinks = [sink_ref[h] for h in range(N_HEADS)]
    gain = g_ref[...]

    @pl.when(i < _N_PROMPT_BLOCKS)
    def _():
        o_ref[...] = _attn_prompt_block(i % (SEQ // WINDOW), sinks, q_ref[...], kvc_ref[...], kvp_ref[...], gain)

    @pl.when(i == _N_PROMPT_BLOCKS)
    def _():
        def chunk(c, carry):
            rows = pl.ds(pl.multiple_of(c * _ROWS_S, _ROWS_S), _ROWS_S)
            seqs = pl.ds(c * _SEQ_PER_CHUNK, _SEQ_PER_CHUNK)
            kc = kc_ref[seqs].reshape(_SEQ_PER_CHUNK * WINDOW, KV_WIDTH)
            vc = vc_ref[seqs].reshape(_SEQ_PER_CHUNK * WINDOW, KV_WIDTH)
            o_ref[rows, :] = _attn_sample_chunk(sinks, q_ref[rows, :], kvc_ref[rows, :], kc, vc, gain)
            return carry
        lax.fori_loop(0, DEC_BATCH // _SEQ_PER_CHUNK, chunk, 0)


def _attention(sink, q_all, kv_all, kc, vc, g, l, casts=()):
    nb = SEQ // WINDOW
    cache = pl.BlockSpec((None, DEC_BATCH, WINDOW, KV_WIDTH), lambda i: (l, 0, 0, 0), pipeline_mode=pl.Buffered(1))
    in_specs = [
        pl.BlockSpec(memory_space=pltpu.SMEM),
        pl.BlockSpec((WINDOW, ATTN_WIDTH), lambda i: (i, 0)),
        pl.BlockSpec((WINDOW, 2 * KV_WIDTH), lambda i: (i, 0)),
        pl.BlockSpec((WINDOW, 2 * KV_WIDTH), lambda i: (jnp.where(i % nb == 0, i, i - 1), 0)),
        cache,
        cache,
        pl.BlockSpec((None, 1, ATTN_WIDTH), lambda i: (l, 0, 0)),
    ]
    out_specs = [pl.BlockSpec((WINDOW, ATTN_WIDTH), lambda i: (i, 0))]
    out_shape = [jax.ShapeDtypeStruct((T_ALL, ATTN_WIDTH), BF16)]
    for w, layer, rows in casts:
        _, n_rows, n_cols = w.shape
        n_blocks = n_rows // rows
        assert n_blocks * rows == n_rows and _N_PROMPT_BLOCKS % n_blocks == 0
        steps = _N_PROMPT_BLOCKS // n_blocks
        blk = lambda i, steps=steps, n_blocks=n_blocks: jnp.minimum(i // steps, n_blocks - 1)
        in_specs.append(pl.BlockSpec((None, rows, n_cols), lambda i, blk=blk, layer=layer: (layer, blk(i), 0)))
        out_specs.append(pl.BlockSpec((rows, n_cols), lambda i, blk=blk: (blk(i), 0)))
        out_shape.append(jax.ShapeDtypeStruct((n_rows, n_cols), BF16))
    return pl.pallas_call(
        functools.partial(_attn_kernel, len(casts)),
        grid=(_N_PROMPT_BLOCKS + 1,),
        in_specs=in_specs,
        out_specs=out_specs,
        out_shape=out_shape,
        compiler_params=_cparams("arbitrary"),
        name="attention",
    )(sink, q_all, kv_all, kv_all, kc, vc, g, *[w for w, _, _ in casts])


def _expand_groups(m):
    m2 = m.reshape(SLAB, SSM_STATE)
    t = jnp.concatenate([m2] * GROUPS_PER_SLAB, axis=1)
    r = lax.broadcasted_iota(jnp.int32, (SLAB, SLAB_STATE), 0) // SSM_GROUP
    c = lax.broadcasted_iota(jnp.int32, (SLAB, SLAB_STATE), 1) // SSM_STATE
    return jnp.where(r == c, t, 0.0)


def _fold_groups(x):
    t = [x[:, i * SLAB:(i + 1) * SLAB] for i in range(2 * SLAB_STATE // SLAB)]
    return jnp.concatenate([t[0] + t[1] + t[2] + t[3], t[4] + t[5] + t[6] + t[7]], axis=1)


def _ssm_tables(lam_ref, logdt_ref, bre_ref, bim_ref, cre_ref, cim_ref, max_chunk):
    H = SLAB_STATE
    dt = jnp.exp(logdt_ref[...])
    lam = lam_ref[...]
    lr, li = lam[0:1], lam[1:2]
    n_tau = ((max_chunk + 1 + 7) // 8) * 8
    tau = lax.broadcasted_iota(jnp.int32, (n_tau, H), 0).astype(F32)
    mag = jnp.exp(tau * (lr * dt))
    ang = tau * (li * dt)
    pw_r, pw_i = mag * jnp.cos(ang), mag * jnp.sin(ang)
    abr, abi = pw_r[1:2], pw_i[1:2]
    nr = abr - 1.0
    den = lr * lr + li * li
    cr = (nr * lr + abi * li) / den
    ci = (abi * lr - nr * li) / den
    xb_r, xb_i = _expand_groups(bre_ref[...]), _expand_groups(bim_ref[...])
    xc_r, xc_i = _expand_groups(cre_ref[...]), _expand_groups(cim_ref[...])
    return pw_r, pw_i, cr, ci, xb_r, xb_i, xc_r, xc_i


def _ssm_operators(L, tables, tt_ref, bst_ref, cst_ref):
    pw_r, pw_i, cr, ci, xb_r, xb_i, xc_r, xc_i = tables
    ccm = _fold_groups(jnp.concatenate([xc_r, -xc_i], axis=1))
    same_group = (lax.broadcasted_iota(jnp.int32, (SLAB, SLAB), 0) // SSM_GROUP
                  == lax.broadcasted_iota(jnp.int32, (SLAB, SLAB), 1) // SSM_GROUP)

    k_lag = [None] * L
    for s in range(L):
        e = L - 1 - s
        w_r = cr * pw_r[e:e + 1] - ci * pw_i[e:e + 1]
        w_i = cr * pw_i[e:e + 1] + ci * pw_r[e:e + 1]
        slab = jnp.concatenate([xb_r * w_r - xb_i * w_i, xb_r * w_i + xb_i * w_r], axis=1)
        bst_ref[s * SLAB:(s + 1) * SLAB, :] = slab.astype(BF16)
        k_all = lax.dot_general(_fold_groups(slab), ccm, _TRANS_B, preferred_element_type=F32, precision=_HI)
        k_lag[e] = jnp.where(same_group, k_all, 0.0)
    for s in range(L):
        for t_lo in range(2):
            lag = L - 2 + t_lo - s
            blk = k_lag[lag] if lag >= 0 else jnp.zeros((SLAB, SLAB), F32)
            tt_ref[s * SLAB:(s + 1) * SLAB, t_lo * SLAB:(t_lo + 1) * SLAB] = blk.astype(BF16)
    for t in range(L):
        p_r, p_i = pw_r[t + 1:t + 2], pw_i[t + 1:t + 2]
        g_t = jnp.concatenate([xc_r * p_r - xc_i * p_i, -(xc_r * p_i + xc_i * p_r)], axis=1)
        cst_ref[:, t * SLAB:(t + 1) * SLAB] = g_t.T.astype(BF16)
    return pw_r[L:L + 1], pw_i[L:L + 1]


_SCAN_ROWS = 8


def _scan_chunk_states(s_r, s_i, a_r, a_i, n_per_seq):
    n, H = s_r.shape
    j8 = lax.broadcasted_iota(jnp.int32, (n, H), 0) % _SCAN_ROWS
    p_r, p_i = a_r, a_i
    d = 1
    while d < _SCAN_ROWS:
        keep = j8 >= d
        sh_r = jnp.where(keep, pltpu.roll(s_r, d, axis=0), 0.0)
        sh_i = jnp.where(keep, pltpu.roll(s_i, d, axis=0), 0.0)
        s_r, s_i = s_r + sh_r * p_r - sh_i * p_i, s_i + sh_r * p_i + sh_i * p_r
        p_r, p_i = p_r * p_r - p_i * p_i, 2.0 * p_r * p_i
        d *= 2
    rows_r, rows_i = [a_r], [a_i]
    for _ in range(_SCAN_ROWS - 1):
        q_r, q_i = rows_r[-1], rows_i[-1]
        rows_r.append(q_r * a_r - q_i * a_i)
        rows_i.append(q_r * a_i + q_i * a_r)
    t_r, t_i = jnp.concatenate(rows_r, axis=0), jnp.concatenate(rows_i, axis=0)
    out_r, out_i = [], []
    for g in range(n // _SCAN_ROWS):
        b_r = s_r[g * _SCAN_ROWS:(g + 1) * _SCAN_ROWS]
        b_i = s_i[g * _SCAN_ROWS:(g + 1) * _SCAN_ROWS]
        if (g * _SCAN_ROWS) % n_per_seq != 0:
            c_r, c_i = out_r[-1][_SCAN_ROWS - 1:], out_i[-1][_SCAN_ROWS - 1:]
            b_r, b_i = b_r + t_r * c_r - t_i * c_i, b_i + t_r * c_i + t_i * c_r
        out_r.append(b_r)
        out_i.append(b_i)
    return jnp.concatenate(out_r, axis=0), jnp.concatenate(out_i, axis=0)


def _ssm_rows(u_ref, y_ref, row0, n_rows, L, ops, n_per_seq=None, h0=None):
    tt_ref, bst_ref, cst_ref, (a_r, a_i) = ops
    H = SLAB_STATE
    nj = n_rows // L
    lhs = jnp.concatenate([u_ref[pl.ds(row0 + s, nj, stride=L), :].astype(BF16) for s in range(L)], axis=1)
    e = jnp.dot(lhs, bst_ref[...], preferred_element_type=F32)
    s_r, s_i = e[:, :H], e[:, H:]
    if h0 is not None:
        in_r, in_i = h0
        s_r = s_r + in_r * a_r - in_i * a_i
        s_i = s_i + in_r * a_i + in_i * a_r
    else:
        s_r, s_i = _scan_chunk_states(s_r, s_i, a_r, a_i, n_per_seq)
        jj = lax.broadcasted_iota(jnp.int32, (nj, H), 0) % n_per_seq
        in_r = jnp.where(jj >= 1, pltpu.roll(s_r, 1, axis=0), 0.0)
        in_i = jnp.where(jj >= 1, pltpu.roll(s_i, 1, axis=0), 0.0)
    y_st = jnp.dot(jnp.concatenate([in_r, in_i], axis=1).astype(BF16), cst_ref[...], preferred_element_type=F32)
    for tp in range(L // 2):
        k = 2 * SLAB * (tp + 1)
        y = jnp.dot(lhs[:, :k], tt_ref[2 * SLAB * (L // 2 - 1 - tp):, :], preferred_element_type=F32)
        y = y + y_st[:, 2 * SLAB * tp:2 * SLAB * (tp + 1)]
        y_ref[pl.ds(row0 + 2 * tp, nj, stride=L), :] = y[:, :SLAB]
        y_ref[pl.ds(row0 + 2 * tp + 1, nj, stride=L), :] = y[:, SLAB:]
    return s_r, s_i


def _ssm_kernel(u_ref, h0r_ref, h0i_ref, lam_ref, logdt_ref, bre_ref, bim_ref, cre_ref, cim_ref,
                y_ref, pr_ref, pi_ref, sr_ref, si_ref,
                tt_p, bst_p, cst_p, tt_s, bst_s, cst_s):
    tables = _ssm_tables(lam_ref, logdt_ref, bre_ref, bim_ref, cre_ref, cim_ref, max(CHUNK_P, CHUNK_S))
    al_p = _ssm_operators(CHUNK_P, tables, tt_p, bst_p, cst_p)
    al_s = _ssm_operators(CHUNK_S, tables, tt_s, bst_s, cst_s)
    n_per_seq = SEQ // CHUNK_P
    s_r, s_i = _ssm_rows(u_ref, y_ref, 0, T_PROMPT, CHUNK_P, (tt_p, bst_p, cst_p, al_p), n_per_seq=n_per_seq)
    for b in range(BATCH):
        last = (b + 1) * n_per_seq - 1
        pr_ref[b:b + 1, :] = s_r[last:last + 1]
        pi_ref[b:b + 1, :] = s_i[last:last + 1]
    n_r, n_i = _ssm_rows(u_ref, y_ref, T_PROMPT, T_SAMPLE, CHUNK_S, (tt_s, bst_s, cst_s, al_s),
                         h0=(h0r_ref[...], h0i_ref[...]))
    sr_ref[...] = n_r
    si_ref[...] = n_i


def _ssm_operator_scratch(L):
    H = SLAB_STATE
    return [pltpu.VMEM((L * SLAB, 2 * SLAB), BF16), pltpu.VMEM((L * SLAB, 2 * H), BF16),
            pltpu.VMEM((2 * H, L * SLAB), BF16)]


def _ssm(u_all, h0_r, h0_i, lam, logdt, bt_re, bt_im, c_re, c_im):
    H = SLAB_STATE
    slab = lambda rows: pl.BlockSpec((rows, SLAB), lambda q: (0, q))
    state = lambda rows: pl.BlockSpec((rows, H), lambda q: (0, q))
    q3 = lambda *blk: pl.BlockSpec((None,) + blk, lambda q: (q, 0, 0))
    grp = pl.BlockSpec((GROUPS_PER_SLAB, SSM_GROUP, SSM_STATE), lambda q: (q, 0, 0))
    return pl.pallas_call(
        _ssm_kernel,
        grid=(N_SLABS,),
        in_specs=[slab(T_ALL), state(DEC_BATCH), state(DEC_BATCH), q3(2, H), q3(1, H), grp, grp, grp, grp],
        out_specs=[slab(T_ALL), state(BATCH), state(BATCH), state(DEC_BATCH), state(DEC_BATCH)],
        out_shape=[
            jax.ShapeDtypeStruct((T_ALL, SSM_WIDTH), F32),
            jax.ShapeDtypeStruct((BATCH, N_SLABS * H), F32),
            jax.ShapeDtypeStruct((BATCH, N_SLABS * H), F32),
            jax.ShapeDtypeStruct((DEC_BATCH, N_SLABS * H), F32),
            jax.ShapeDtypeStruct((DEC_BATCH, N_SLABS * H), F32),
        ],
        scratch_shapes=_ssm_operator_scratch(CHUNK_P) + _ssm_operator_scratch(CHUNK_S),
        compiler_params=_cparams("parallel"),
        name="ssm",
    )(u_all, h0_r, h0_i, lam, logdt, bt_re, bt_im, c_re, c_im)


MIX_TILE = 640


def _mix_kernel(y_ref, u_ref, d_ref, wg_ref, gs_ref, a_ref, wo_ref, x_ref, o_ref):
    mixed = jnp.dot(a_ref[...], wo_ref[:ATTN_WIDTH, :], preferred_element_type=F32)
    y = y_ref[...] + d_ref[...] * u_ref[...]
    z = 0.5 * y * (1.0 + lax.erf(y * (1.0 / math.sqrt(2.0))))
    gate = jnp.dot(z.astype(BF16), wg_ref[...], preferred_element_type=F32)
    s_n = _rms(z * jax.nn.sigmoid(gate), gs_ref[...]).astype(BF16)
    mixed = mixed + jnp.dot(s_n, wo_ref[ATTN_WIDTH:, :], preferred_element_type=F32)
    o_ref[...] = x_ref[...] + mixed


def _mix(y_all, u_all, d, w_glu_b, g_ssm, a_n, w_out_b, x_all, l):
    row = lambda i: (i, 0)
    par = lambda i: (l, 0, 0)
    whole = lambda shape: pl.BlockSpec(shape, lambda i: (0, 0), pipeline_mode=pl.Buffered(1))
    return pl.pallas_call(
        _mix_kernel,
        grid=(T_ALL // MIX_TILE,),
        in_specs=[
            pl.BlockSpec((MIX_TILE, SSM_WIDTH), row),
            pl.BlockSpec((MIX_TILE, SSM_WIDTH), row),
            pl.BlockSpec((None, 1, SSM_WIDTH), par),
            whole((SSM_WIDTH, SSM_WIDTH)),
            pl.BlockSpec((None, 1, SSM_WIDTH), par),
            pl.BlockSpec((MIX_TILE, ATTN_WIDTH), row),
            whole((D_MODEL, D_MODEL)),
            pl.BlockSpec((MIX_TILE, D_MODEL), row),
        ],
        out_specs=pl.BlockSpec((MIX_TILE, D_MODEL), row),
        out_shape=jax.ShapeDtypeStruct((T_ALL, D_MODEL), F32),
        compiler_params=_cparams("parallel"),
        name="mix",
    )(y_all, u_all, d, w_glu_b, g_ssm, a_n, w_out_b, x_all)


def _ffn_body(x_ref, g_ref, wg_ref, wu_ref, wd_ref, o_ref, h_ref):
    @pl.when(pl.program_id(1) == 0)
    def _():
        x = x_ref[...]
        h_ref[...] = _rms(x, g_ref[...]).astype(BF16)
        o_ref[...] = x

    h = h_ref[...]
    gate = jnp.dot(h, wg_ref[...], preferred_element_type=F32)
    up = jnp.dot(h, wu_ref[...], preferred_element_type=F32)
    act = (gate * jax.nn.sigmoid(gate) * up).astype(BF16)
    o_ref[...] += jnp.dot(act, wd_ref[...], preferred_element_type=F32)


def _ffn_final_kernel(x_ref, g_ref, wg_ref, wu_ref, wd_ref, gf_ref, yp_ref, ys_ref, h_ref):
    _ffn_body(x_ref, g_ref, wg_ref, wu_ref, wd_ref, yp_ref, h_ref)

    @pl.when(pl.program_id(1) == _N_FF_TILES - 1)
    def _():
        y = _rms(yp_ref[...], gf_ref[...])
        yp_ref[...] = y

        @pl.when(pl.program_id(0) == _N_ROW_TILES - 1)
        def _():
            ys_ref[...] = y[ROW_TILE - T_SAMPLE:]


def _ffn_cast_kernel(x_ref, g_ref, wg_ref, wu_ref, wd_ref, ng_ref, nu_ref, nd_ref,
                     o_ref, ngb_ref, nub_ref, ndb_ref, h_ref):
    _ffn_body(x_ref, g_ref, wg_ref, wu_ref, wd_ref, o_ref, h_ref)
    ngb_ref[...] = ng_ref[...].astype(BF16)
    nub_ref[...] = nu_ref[...].astype(BF16)
    ndb_ref[...] = nd_ref[...].astype(BF16)


_N_ROW_TILES = T_ALL // ROW_TILE
_N_FF_TILES = D_FF // FF_TILE
_CAST_ROWS = D_MODEL // _N_ROW_TILES


def _ffn(x_all, g, w_gate_b, w_up_b, w_down_b, l, w_next=None, g_final=None):
    in_specs = [
        pl.BlockSpec((ROW_TILE, D_MODEL), lambda i, f: (i, 0)),
        pl.BlockSpec((None, 1, D_MODEL), lambda i, f: (l, 0, 0)),
        pl.BlockSpec((D_MODEL, FF_TILE), lambda i, f: (0, f)),
        pl.BlockSpec((D_MODEL, FF_TILE), lambda i, f: (0, f)),
        pl.BlockSpec((FF_TILE, D_MODEL), lambda i, f: (f, 0)),
    ]
    out_specs = [pl.BlockSpec((ROW_TILE, D_MODEL), lambda i, f: (i, 0))]
    out_shape = [jax.ShapeDtypeStruct((T_ALL, D_MODEL), F32)]
    args = [x_all, g, w_gate_b, w_up_b, w_down_b]
    if w_next is not None:
        assert D_MODEL % _N_ROW_TILES == 0
        in_specs += [
            pl.BlockSpec((None, _CAST_ROWS, FF_TILE), lambda i, f: (l + 1, i, f)),
            pl.BlockSpec((None, _CAST_ROWS, FF_TILE), lambda i, f: (l + 1, i, f)),
            pl.BlockSpec((None, FF_TILE, _CAST_ROWS), lambda i, f: (l + 1, f, i)),
        ]
        out_specs += [
            pl.BlockSpec((_CAST_ROWS, FF_TILE), lambda i, f: (i, f)),
            pl.BlockSpec((_CAST_ROWS, FF_TILE), lambda i, f: (i, f)),
            pl.BlockSpec((FF_TILE, _CAST_ROWS), lambda i, f: (f, i)),
        ]
        out_shape += [jax.ShapeDtypeStruct((D_MODEL, D_FF), BF16), jax.ShapeDtypeStruct((D_MODEL, D_FF), BF16),
                      jax.ShapeDtypeStruct((D_FF, D_MODEL), BF16)]
        args += list(w_next)
    else:
        assert T_PROMPT > (_N_ROW_TILES - 1) * ROW_TILE and T_ALL == _N_ROW_TILES * ROW_TILE
        in_specs.append(pl.BlockSpec((1, D_MODEL), lambda i, f: (0, 0)))
        out_specs = [pl.BlockSpec((ROW_TILE, D_MODEL), lambda i, f: (i, 0)),
                     pl.BlockSpec((T_SAMPLE, D_MODEL), lambda i, f: (0, 0))]
        out_shape = [jax.ShapeDtypeStruct((T_PROMPT, D_MODEL), F32), jax.ShapeDtypeStruct((T_SAMPLE, D_MODEL), F32)]
        args.append(g_final)
    return pl.pallas_call(
        _ffn_final_kernel if w_next is None else _ffn_cast_kernel,
        grid=(_N_ROW_TILES, _N_FF_TILES),
        in_specs=in_specs,
        out_specs=out_specs,
        out_shape=out_shape,
        scratch_shapes=[pltpu.VMEM((ROW_TILE, D_MODEL), BF16)],
        compiler_params=_cparams("parallel" if w_next is not None else "arbitrary", "arbitrary"),
        name="ffn",
    )(*args)


def kernel(x_prompt, x_sample, cache_k, cache_v, state_ssm_re, state_ssm_im, norm_mix, w_in, attn_sink, ssm_a_re,
           ssm_a_im, ssm_log_dt, ssm_b_re, ssm_b_im, ssm_c_re, ssm_c_im, ssm_d, w_glu, norm_attn_out, norm_ssm_out,
           w_out, norm_ffn, w_gate, w_up, w_down, norm_final):
    G, P, H = N_SSM_GROUPS, SSM_STATE, SLAB_STATE
    w_in_b, w_out_b, w_glu_b = w_in[0].astype(BF16), w_out[0].astype(BF16), w_glu[0].astype(BF16)
    row3 = lambda p: p.reshape(DEPTH, 1, -1)
    g_mix, g_attn, g_ssm, g_ffn, d3 = row3(norm_mix), row3(norm_attn_out), row3(norm_ssm_out), row3(norm_ffn), row3(ssm_d)
    cache_k2 = cache_k.reshape(DEPTH, DEC_BATCH, WINDOW, KV_WIDTH)
    cache_v2 = cache_v.reshape(DEPTH, DEC_BATCH, WINDOW, KV_WIDTH)

    k_p, v_p, hr_p, hi_p, k_s, v_s, hr_s, hi_s = [], [], [], [], [], [], [], []
    for l in range(DEPTH):
        if l == 0:
            q_all, kv_all, u_all, x_all = _in_proj_first(x_prompt.reshape(T_PROMPT, D_MODEL),
                                                         x_sample.reshape(T_SAMPLE, D_MODEL), g_mix, w_in_b)
        else:
            q_all, kv_all, u_all = _in_proj(x_all, g_mix, w_in_b, l)

        casts = []
        if l == 0:
            casts += [(w_gate, 0, D_MODEL // 64), (w_up, 0, D_MODEL // 64), (w_down, 0, D_FF // 32)]
        if l + 1 < DEPTH:
            casts += [(w_in, l + 1, D_MODEL // 64), (w_out, l + 1, D_MODEL // 64), (w_glu, l + 1, SSM_WIDTH // 64)]
        a_n, *cast_out = _attention(attn_sink[l], q_all, kv_all, cache_k2, cache_v2, g_attn, l, casts)
        if l == 0:
            ffn_w, cast_out = cast_out[:3], cast_out[3:]
        w_glu_cur, w_out_cur = w_glu_b, w_out_b
        if l + 1 < DEPTH:
            w_in_b, w_out_b, w_glu_b = cast_out

        lam = jnp.stack([ssm_a_re[l].reshape(N_SLABS, H), ssm_a_im[l].reshape(N_SLABS, H)], axis=1)
        logdt = jnp.repeat(ssm_log_dt[l], P).reshape(N_SLABS, 1, H)
        y_all, sfin_r, sfin_i, snew_r, snew_i = _ssm(
            u_all, state_ssm_re[l].reshape(DEC_BATCH, G * P), state_ssm_im[l].reshape(DEC_BATCH, G * P),
            lam, logdt, ssm_b_re[l].transpose(0, 2, 1), ssm_b_im[l].transpose(0, 2, 1), ssm_c_re[l], ssm_c_im[l])
        x_all = _mix(y_all, u_all, d3, w_glu_cur, g_ssm, a_n, w_out_cur, x_all, l)
        if l + 1 < DEPTH:
            x_all, *ffn_w = _ffn(x_all, g_ffn, *ffn_w, l, w_next=(w_gate, w_up, w_down))
        else:
            y_prompt, y_sample = _ffn(x_all, g_ffn, *ffn_w, l, g_final=norm_final.reshape(1, D_MODEL))

        kv_p = jnp.stack([kv_all[(b + 1) * SEQ - WINDOW:(b + 1) * SEQ] for b in range(BATCH)])
        kv_p = kv_p.reshape(BATCH, WINDOW, 2, N_KV_HEADS, HEAD_DIM)
        k_p.append(kv_p[:, :, 0])
        v_p.append(kv_p[:, :, 1])
        kv_s = kv_all[T_PROMPT:].reshape(DEC_BATCH, DEC_SEQ, 2, N_KV_HEADS, HEAD_DIM)
        k_s.append(jnp.concatenate([cache_k[l][:, DEC_SEQ:], kv_s[:, :, 0]], axis=1))
        v_s.append(jnp.concatenate([cache_v[l][:, DEC_SEQ:], kv_s[:, :, 1]], axis=1))
        hr_p.append(sfin_r.reshape(BATCH, G, P))
        hi_p.append(sfin_i.reshape(BATCH, G, P))
        hr_s.append(snew_r.reshape(DEC_BATCH, G, P))
        hi_s.append(snew_i.reshape(DEC_BATCH, G, P))

    y_prompt = y_prompt.reshape(BATCH, SEQ, D_MODEL)
    y_sample = y_sample.reshape(DEC_BATCH, DEC_SEQ, D_MODEL)
    st = jnp.stack
    return (y_prompt, y_sample, st(k_p), st(v_p), st(hr_p), st(hi_p), st(k_s), st(v_s), st(hr_s), st(hi_s))
```

```python
import functools
import math

import jax
import jax.numpy as jnp
from jax import lax
from jax.experimental import pallas as pl
from jax.experimental.pallas import tpu as pltpu

F32 = jnp.float32
BF16 = jnp.bfloat16

D_MODEL = 2048
BATCH = 2
SEQ = 4096
DEPTH = 4
DEC_BATCH = 32
DEC_SEQ = 4
ATTN_WIDTH = 1024
SSM_WIDTH = 1024
HEAD_DIM = 64
N_HEADS = 16
N_KV_HEADS = 2
GQA_GROUP = 8
KV_WIDTH = 128
WINDOW = 128
SSM_GROUP = 16
N_SSM_GROUPS = 64
SSM_STATE = 64
D_IN = ATTN_WIDTH + 2 * KV_WIDTH + SSM_WIDTH
D_FF = 5632
EPS = 1e-5

T_PROMPT = BATCH * SEQ
T_SAMPLE = DEC_BATCH * DEC_SEQ
T_ALL = T_PROMPT + T_SAMPLE
ROW_TILE = 1040
FF_TILE = 512
CHUNK_P = 16
CHUNK_S = DEC_SEQ
SLAB = 128
GROUPS_PER_SLAB = SLAB // SSM_GROUP
N_SLABS = SSM_WIDTH // SLAB
SLAB_STATE = GROUPS_PER_SLAB * SSM_STATE
MASK_NEG = -1e30
V7X_VMEM_BYTES = 64 * 1024 * 1024
VMEM_LIMIT = V7X_VMEM_BYTES - 4 * 1024 * 1024

_TRANS_B = (((1,), (1,)), ((), ()))
_HI = lax.Precision.HIGHEST


def _cparams(*sem):
    return pltpu.CompilerParams(dimension_semantics=sem, vmem_limit_bytes=VMEM_LIMIT)


def _rms(x, g):
    ms = jnp.mean(x * x, axis=-1, keepdims=True)
    return x * lax.rsqrt(ms + EPS) * g


def _in_proj_kernel(x_ref, g_ref, w_ref, q_ref, kv_ref, u_ref):
    h = _rms(x_ref[...], g_ref[...]).astype(BF16)
    z = jnp.dot(h, w_ref[...], preferred_element_type=F32)
    q_ref[...] = (z[:, :ATTN_WIDTH] * (HEAD_DIM ** -0.5)).astype(BF16)
    kv_ref[...] = z[:, ATTN_WIDTH:ATTN_WIDTH + 2 * KV_WIDTH]
    u_ref[...] = z[:, ATTN_WIDTH + 2 * KV_WIDTH:]


def _in_proj_out(tile):
    row = lambda i: (i, 0)
    specs = [pl.BlockSpec((tile, ATTN_WIDTH), row), pl.BlockSpec((tile, 2 * KV_WIDTH), row),
             pl.BlockSpec((tile, SSM_WIDTH), row)]
    shapes = [jax.ShapeDtypeStruct((T_ALL, ATTN_WIDTH), BF16), jax.ShapeDtypeStruct((T_ALL, 2 * KV_WIDTH), F32),
              jax.ShapeDtypeStruct((T_ALL, SSM_WIDTH), F32)]
    return specs, shapes


def _in_proj(x_all, g, w_in_b, l):
    out_specs, out_shapes = _in_proj_out(ROW_TILE)
    return pl.pallas_call(
        _in_proj_kernel,
        grid=(T_ALL // ROW_TILE,),
        in_specs=[
            pl.BlockSpec((ROW_TILE, D_MODEL), lambda i: (i, 0)),
            pl.BlockSpec((None, 1, D_MODEL), lambda i: (l, 0, 0)),
            pl.BlockSpec((D_MODEL, D_IN), lambda i: (0, 0), pipeline_mode=pl.Buffered(1)),
        ],
        out_specs=out_specs,
        out_shape=out_shapes,
        compiler_params=_cparams("parallel"),
        name="in_proj",
    )(x_all, g, w_in_b)


FIRST_TILE = 640
_LAST_TILE = T_ALL // FIRST_TILE - 1
_PROMPT_ROWS_LAST_TILE = T_PROMPT - _LAST_TILE * FIRST_TILE


def _in_proj_first_kernel(xp_ref, xs_ref, g_ref, w_ref, q_ref, kv_ref, u_ref, x_ref):
    i = pl.program_id(0)

    @pl.when(i < _LAST_TILE)
    def _():
        x_ref[...] = xp_ref[...]

    @pl.when(i == _LAST_TILE)
    def _():
        x_ref[:_PROMPT_ROWS_LAST_TILE, :] = xp_ref[:_PROMPT_ROWS_LAST_TILE, :]
        x_ref[_PROMPT_ROWS_LAST_TILE:, :] = xs_ref[...]

    _in_proj_kernel(x_ref, g_ref, w_ref, q_ref, kv_ref, u_ref)


def _in_proj_first(x_prompt, x_sample, g, w_in_b):
    assert FIRST_TILE - _PROMPT_ROWS_LAST_TILE == T_SAMPLE
    out_specs, out_shapes = _in_proj_out(FIRST_TILE)
    return pl.pallas_call(
        _in_proj_first_kernel,
        grid=(T_ALL // FIRST_TILE,),
        in_specs=[
            pl.BlockSpec((FIRST_TILE, D_MODEL), lambda i: (i, 0)),
            pl.BlockSpec((T_SAMPLE, D_MODEL), lambda i: (0, 0)),
            pl.BlockSpec((None, 1, D_MODEL), lambda i: (0, 0, 0)),
            pl.BlockSpec((D_MODEL, D_IN), lambda i: (0, 0), pipeline_mode=pl.Buffered(1)),
        ],
        out_specs=out_specs + [pl.BlockSpec((FIRST_TILE, D_MODEL), lambda i: (i, 0))],
        out_shape=out_shapes + [jax.ShapeDtypeStruct((T_ALL, D_MODEL), F32)],
        compiler_params=_cparams("parallel"),
        name="in_proj_first",
    )(x_prompt, x_sample, g, w_in_b)


def _softmax_pv(s_parts, bias_parts, v_parts, sinks, rows):
    p_parts = [[] for _ in s_parts]
    dens = []
    for g in range(GQA_GROUP):
        sl = slice(g * rows, (g + 1) * rows)
        sk = sinks[g]
        sg = [s[sl] + b for s, b in zip(s_parts, bias_parts)]
        m = sk
        for s in sg:
            m = jnp.maximum(jnp.max(s, axis=-1, keepdims=True), m)
        den = jnp.exp(sk - m)
        for j, s in enumerate(sg):
            p = jnp.exp(s - m)
            den = den + jnp.sum(p, axis=-1, keepdims=True)
            p_parts[j].append(p.astype(BF16))
        dens.append(den)
    o = None
    for j, v in enumerate(v_parts):
        pj = jnp.concatenate(p_parts[j], axis=0)
        oj = jnp.dot(pj, v, preferred_element_type=F32)
        o = oj if o is None else o + oj
    return [o[g * rows:(g + 1) * rows] / dens[g] for g in range(GQA_GROUP)]


def _head_cols(h):
    return slice(h * HEAD_DIM, (h + 1) * HEAD_DIM)


def _stack_heads(q, hk):
    return jnp.concatenate([q[:, _head_cols(hk * GQA_GROUP + g)] for g in range(GQA_GROUP)], axis=0)


def _attn_prompt_block(n, sinks, q, kvc, kvp):
    row = lax.broadcasted_iota(jnp.int32, (WINDOW, 2 * WINDOW), 0)
    col = lax.broadcasted_iota(jnp.int32, (WINDOW, 2 * WINDOW), 1)
    ok = (col > row) & (col <= row + WINDOW) & ((col >= WINDOW) | (n > 0))
    bias = jnp.where(ok, 0.0, MASK_NEG).astype(F32)
    outs = []
    for hk in range(N_KV_HEADS):
        ks, vs = _head_cols(hk), _head_cols(N_KV_HEADS + hk)
        k = jnp.concatenate([kvp[:, ks], kvc[:, ks]], axis=0).astype(BF16)
        v = jnp.concatenate([kvp[:, vs], kvc[:, vs]], axis=0).astype(BF16)
        for g in range(GQA_GROUP):
            h = hk * GQA_GROUP + g
            s = lax.dot_general(q[:, _head_cols(h)], k, _TRANS_B, preferred_element_type=F32) + bias
            m = jnp.maximum(jnp.max(s, axis=-1, keepdims=True), sinks[h])
            p = jnp.exp(s - m)
            den = jnp.sum(p, axis=-1, keepdims=True) + jnp.exp(sinks[h] - m)
            outs.append(jnp.dot(p.astype(BF16), v, preferred_element_type=F32) / den)
    return jnp.concatenate(outs, axis=1)


_SEQ_PER_CHUNK = 4
_ROWS_S = _SEQ_PER_CHUNK * DEC_SEQ


def _attn_sample_chunk(sinks, q, kv, kc, vc):
    r = lax.broadcasted_iota(jnp.int32, (_ROWS_S, _SEQ_PER_CHUNK * WINDOW), 0)
    c = lax.broadcasted_iota(jnp.int32, (_ROWS_S, _SEQ_PER_CHUNK * WINDOW), 1)
    ok_c = (c // WINDOW == r // DEC_SEQ) & (c % WINDOW > r % DEC_SEQ)
    bias_c = jnp.where(ok_c, 0.0, MASK_NEG).astype(F32)
    r2 = lax.broadcasted_iota(jnp.int32, (_ROWS_S, _ROWS_S), 0)
    c2 = lax.broadcasted_iota(jnp.int32, (_ROWS_S, _ROWS_S), 1)
    ok_n = (c2 // DEC_SEQ == r2 // DEC_SEQ) & (c2 % DEC_SEQ <= r2 % DEC_SEQ)
    bias_n = jnp.where(ok_n, 0.0, MASK_NEG).astype(F32)
    outs = []
    for hk in range(N_KV_HEADS):
        ks, vs = _head_cols(hk), _head_cols(N_KV_HEADS + hk)
        k_c, v_c = kc[:, ks].astype(BF16), vc[:, ks].astype(BF16)
        k_n, v_n = kv[:, ks].astype(BF16), kv[:, vs].astype(BF16)
        qs = _stack_heads(q, hk)
        s_c = lax.dot_general(qs, k_c, _TRANS_B, preferred_element_type=F32)
        s_n = lax.dot_general(qs, k_n, _TRANS_B, preferred_element_type=F32)
        outs += _softmax_pv([s_c, s_n], [bias_c, bias_n], [v_c, v_n],
                            sinks[hk * GQA_GROUP:(hk + 1) * GQA_GROUP], _ROWS_S)
    return jnp.concatenate(outs, axis=1)


_N_PROMPT_BLOCKS = T_PROMPT // WINDOW


def _attn_kernel(n_cast, sink_ref, q_ref, kvc_ref, kvp_ref, kc_ref, vc_ref, *rest):
    o_ref = rest[n_cast]
    for src_ref, dst_ref in zip(rest[:n_cast], rest[n_cast + 1:]):
        dst_ref[...] = src_ref[...].astype(BF16)
    i = pl.program_id(0)
    sinks = [sink_ref[h] for h in range(N_HEADS)]

    @pl.when(i < _N_PROMPT_BLOCKS)
    def _():
        o_ref[...] = _attn_prompt_block(i % (SEQ // WINDOW), sinks, q_ref[...], kvc_ref[...], kvp_ref[...])

    @pl.when(i == _N_PROMPT_BLOCKS)
    def _():
        def chunk(c, carry):
            rows = pl.ds(pl.multiple_of(c * _ROWS_S, _ROWS_S), _ROWS_S)
            seqs = pl.ds(c * _SEQ_PER_CHUNK, _SEQ_PER_CHUNK)
            kc = kc_ref[seqs].reshape(_SEQ_PER_CHUNK * WINDOW, KV_WIDTH)
            vc = vc_ref[seqs].reshape(_SEQ_PER_CHUNK * WINDOW, KV_WIDTH)
            o_ref[rows, :] = _attn_sample_chunk(sinks, q_ref[rows, :], kvc_ref[rows, :], kc, vc)
            return carry
        lax.fori_loop(0, DEC_BATCH // _SEQ_PER_CHUNK, chunk, 0)


def _attention(sink, q_all, kv_all, kc, vc, l, casts=()):
    nb = SEQ // WINDOW
    cache = pl.BlockSpec((None, DEC_BATCH, WINDOW, KV_WIDTH), lambda i: (l, 0, 0, 0), pipeline_mode=pl.Buffered(1))
    in_specs = [
        pl.BlockSpec(memory_space=pltpu.SMEM),
        pl.BlockSpec((WINDOW, ATTN_WIDTH), lambda i: (i, 0)),
        pl.BlockSpec((WINDOW, 2 * KV_WIDTH), lambda i: (i, 0)),
        pl.BlockSpec((WINDOW, 2 * KV_WIDTH), lambda i: (jnp.where(i % nb == 0, i, i - 1), 0)),
        cache,
        cache,
    ]
    out_specs = [pl.BlockSpec((WINDOW, ATTN_WIDTH), lambda i: (i, 0))]
    out_shape = [jax.ShapeDtypeStruct((T_ALL, ATTN_WIDTH), F32)]
    for w, layer, rows in casts:
        _, n_rows, n_cols = w.shape
        n_blocks = n_rows // rows
        assert n_blocks * rows == n_rows and _N_PROMPT_BLOCKS % n_blocks == 0
        steps = _N_PROMPT_BLOCKS // n_blocks
        blk = lambda i, steps=steps, n_blocks=n_blocks: jnp.minimum(i // steps, n_blocks - 1)
        in_specs.append(pl.BlockSpec((None, rows, n_cols), lambda i, blk=blk, layer=layer: (layer, blk(i), 0)))
        out_specs.append(pl.BlockSpec((rows, n_cols), lambda i, blk=blk: (blk(i), 0)))
        out_shape.append(jax.ShapeDtypeStruct((n_rows, n_cols), BF16))
    return pl.pallas_call(
        functools.partial(_attn_kernel, len(casts)),
        grid=(_N_PROMPT_BLOCKS + 1,),
        in_specs=in_specs,
        out_specs=out_specs,
        out_shape=out_shape,
        compiler_params=_cparams("arbitrary"),
        name="attention",
    )(sink, q_all, kv_all, kv_all, kc, vc, *[w for w, _, _ in casts])


def _expand_groups(m):
    m2 = m.reshape(SLAB, SSM_STATE)
    t = jnp.concatenate([m2] * GROUPS_PER_SLAB, axis=1)
    r = lax.broadcasted_iota(jnp.int32, (SLAB, SLAB_STATE), 0) // SSM_GROUP
    c = lax.broadcasted_iota(jnp.int32, (SLAB, SLAB_STATE), 1) // SSM_STATE
    return jnp.where(r == c, t, 0.0)


def _fold_groups(x):
    t = [x[:, i * SLAB:(i + 1) * SLAB] for i in range(2 * SLAB_STATE // SLAB)]
    return jnp.concatenate([t[0] + t[1] + t[2] + t[3], t[4] + t[5] + t[6] + t[7]], axis=1)


def _ssm_tables(lam_ref, logdt_ref, bre_ref, bim_ref, cre_ref, cim_ref, max_chunk):
    H = SLAB_STATE
    dt = jnp.exp(logdt_ref[...])
    lam = lam_ref[...]
    lr, li = lam[0:1], lam[1:2]
    n_tau = ((max_chunk + 1 + 7) // 8) * 8
    tau = lax.broadcasted_iota(jnp.int32, (n_tau, H), 0).astype(F32)
    mag = jnp.exp(tau * (lr * dt))
    ang = tau * (li * dt)
    pw_r, pw_i = mag * jnp.cos(ang), mag * jnp.sin(ang)
    abr, abi = pw_r[1:2], pw_i[1:2]
    nr = abr - 1.0
    den = lr * lr + li * li
    cr = (nr * lr + abi * li) / den
    ci = (abi * lr - nr * li) / den
    xb_r, xb_i = _expand_groups(bre_ref[...]), _expand_groups(bim_ref[...])
    xc_r, xc_i = _expand_groups(cre_ref[...]), _expand_groups(cim_ref[...])
    return pw_r, pw_i, cr, ci, xb_r, xb_i, xc_r, xc_i


def _ssm_operators(L, tables, tt_ref, bst_ref, cst_ref):
    pw_r, pw_i, cr, ci, xb_r, xb_i, xc_r, xc_i = tables
    ccm = _fold_groups(jnp.concatenate([xc_r, -xc_i], axis=1))
    same_group = (lax.broadcasted_iota(jnp.int32, (SLAB, SLAB), 0) // SSM_GROUP
                  == lax.broadcasted_iota(jnp.int32, (SLAB, SLAB), 1) // SSM_GROUP)

    k_lag = [None] * L
    for s in range(L):
        e = L - 1 - s
        w_r = cr * pw_r[e:e + 1] - ci * pw_i[e:e + 1]
        w_i = cr * pw_i[e:e + 1] + ci * pw_r[e:e + 1]
        slab = jnp.concatenate([xb_r * w_r - xb_i * w_i, xb_r * w_i + xb_i * w_r], axis=1)
        bst_ref[s * SLAB:(s + 1) * SLAB, :] = slab.astype(BF16)
        k_all = lax.dot_general(_fold_groups(slab), ccm, _TRANS_B, preferred_element_type=F32, precision=_HI)
        k_lag[e] = jnp.where(same_group, k_all, 0.0)
    for s in range(L):
        for t_lo in range(2):
            lag = L - 2 + t_lo - s
            blk = k_lag[lag] if lag >= 0 else jnp.zeros((SLAB, SLAB), F32)
            tt_ref[s * SLAB:(s + 1) * SLAB, t_lo * SLAB:(t_lo + 1) * SLAB] = blk.astype(BF16)
    for t in range(L):
        p_r, p_i = pw_r[t + 1:t + 2], pw_i[t + 1:t + 2]
        g_t = jnp.concatenate([xc_r * p_r - xc_i * p_i, -(xc_r * p_i + xc_i * p_r)], axis=1)
        cst_ref[:, t * SLAB:(t + 1) * SLAB] = g_t.T.astype(BF16)
    return pw_r[L:L + 1], pw_i[L:L + 1]


_SCAN_ROWS = 8


def _scan_chunk_states(s_r, s_i, a_r, a_i, n_per_seq):
    n, H = s_r.shape
    j8 = lax.broadcasted_iota(jnp.int32, (n, H), 0) % _SCAN_ROWS
    p_r, p_i = a_r, a_i
    d = 1
    while d < _SCAN_ROWS:
        keep = j8 >= d
        sh_r = jnp.where(keep, pltpu.roll(s_r, d, axis=0), 0.0)
        sh_i = jnp.where(keep, pltpu.roll(s_i, d, axis=0), 0.0)
        s_r, s_i = s_r + sh_r * p_r - sh_i * p_i, s_i + sh_r * p_i + sh_i * p_r
        p_r, p_i = p_r * p_r - p_i * p_i, 2.0 * p_r * p_i
        d *= 2
    rows_r, rows_i = [a_r], [a_i]
    for _ in range(_SCAN_ROWS - 1):
        q_r, q_i = rows_r[-1], rows_i[-1]
        rows_r.append(q_r * a_r - q_i * a_i)
        rows_i.append(q_r * a_i + q_i * a_r)
    t_r, t_i = jnp.concatenate(rows_r, axis=0), jnp.concatenate(rows_i, axis=0)
    out_r, out_i = [], []
    for g in range(n // _SCAN_ROWS):
        b_r = s_r[g * _SCAN_ROWS:(g + 1) * _SCAN_ROWS]
        b_i = s_i[g * _SCAN_ROWS:(g + 1) * _SCAN_ROWS]
        if (g * _SCAN_ROWS) % n_per_seq != 0:
            c_r, c_i = out_r[-1][_SCAN_ROWS - 1:], out_i[-1][_SCAN_ROWS - 1:]
            b_r, b_i = b_r + t_r * c_r - t_i * c_i, b_i + t_r * c_i + t_i * c_r
        out_r.append(b_r)
        out_i.append(b_i)
    return jnp.concatenate(out_r, axis=0), jnp.concatenate(out_i, axis=0)


def _ssm_rows(u_ref, y_ref, row0, n_rows, L, ops, n_per_seq=None, h0=None):
    tt_ref, bst_ref, cst_ref, (a_r, a_i) = ops
    H = SLAB_STATE
    nj = n_rows // L
    lhs = jnp.concatenate([u_ref[pl.ds(row0 + s, nj, stride=L), :].astype(BF16) for s in range(L)], axis=1)
    e = jnp.dot(lhs, bst_ref[...], preferred_element_type=F32)
    s_r, s_i = e[:, :H], e[:, H:]
    if h0 is not None:
        in_r, in_i = h0
        s_r = s_r + in_r * a_r - in_i * a_i
        s_i = s_i + in_r * a_i + in_i * a_r
    else:
        s_r, s_i = _scan_chunk_states(s_r, s_i, a_r, a_i, n_per_seq)
        jj = lax.broadcasted_iota(jnp.int32, (nj, H), 0) % n_per_seq
        in_r = jnp.where(jj >= 1, pltpu.roll(s_r, 1, axis=0), 0.0)
        in_i = jnp.where(jj >= 1, pltpu.roll(s_i, 1, axis=0), 0.0)
    y_st = jnp.dot(jnp.concatenate([in_r, in_i], axis=1).astype(BF16), cst_ref[...], preferred_element_type=F32)
    for tp in range(L // 2):
        k = 2 * SLAB * (tp + 1)
        y = jnp.dot(lhs[:, :k], tt_ref[2 * SLAB * (L // 2 - 1 - tp):, :], preferred_element_type=F32)
        y = y + y_st[:, 2 * SLAB * tp:2 * SLAB * (tp + 1)]
        y_ref[pl.ds(row0 + 2 * tp, nj, stride=L), :] = y[:, :SLAB]
        y_ref[pl.ds(row0 + 2 * tp + 1, nj, stride=L), :] = y[:, SLAB:]
    return s_r, s_i


def _ssm_kernel(u_ref, h0r_ref, h0i_ref, lam_ref, logdt_ref, bre_ref, bim_ref, cre_ref, cim_ref,
                y_ref, pr_ref, pi_ref, sr_ref, si_ref,
                tt_p, bst_p, cst_p, tt_s, bst_s, cst_s):
    tables = _ssm_tables(lam_ref, logdt_ref, bre_ref, bim_ref, cre_ref, cim_ref, max(CHUNK_P, CHUNK_S))
    al_p = _ssm_operators(CHUNK_P, tables, tt_p, bst_p, cst_p)
    al_s = _ssm_operators(CHUNK_S, tables, tt_s, bst_s, cst_s)
    n_per_seq = SEQ // CHUNK_P
    s_r, s_i = _ssm_rows(u_ref, y_ref, 0, T_PROMPT, CHUNK_P, (tt_p, bst_p, cst_p, al_p), n_per_seq=n_per_seq)
    for b in range(BATCH):
        last = (b + 1) * n_per_seq - 1
        pr_ref[b:b + 1, :] = s_r[last:last + 1]
        pi_ref[b:b + 1, :] = s_i[last:last + 1]
    n_r, n_i = _ssm_rows(u_ref, y_ref, T_PROMPT, T_SAMPLE, CHUNK_S, (tt_s, bst_s, cst_s, al_s),
                         h0=(h0r_ref[...], h0i_ref[...]))
    sr_ref[...] = n_r
    si_ref[...] = n_i


def _ssm_operator_scratch(L):
    H = SLAB_STATE
    return [pltpu.VMEM((L * SLAB, 2 * SLAB), BF16), pltpu.VMEM((L * SLAB, 2 * H), BF16),
            pltpu.VMEM((2 * H, L * SLAB), BF16)]


def _ssm(u_all, h0_r, h0_i, lam, logdt, bt_re, bt_im, c_re, c_im, l):
    H = SLAB_STATE
    slab = lambda rows: pl.BlockSpec((rows, SLAB), lambda q: (0, q))
    state = lambda rows: pl.BlockSpec((rows, H), lambda q: (0, q))
    q3 = lambda *blk: pl.BlockSpec((None, None) + blk, lambda q: (l, q, 0, 0))
    grp = pl.BlockSpec((None, GROUPS_PER_SLAB, SSM_GROUP, SSM_STATE), lambda q: (l, q, 0, 0))
    return pl.pallas_call(
        _ssm_kernel,
        grid=(N_SLABS,),
        in_specs=[slab(T_ALL), state(DEC_BATCH), state(DEC_BATCH), q3(2, H), q3(1, H), grp, grp, grp, grp],
        out_specs=[slab(T_ALL), state(BATCH), state(BATCH), state(DEC_BATCH), state(DEC_BATCH)],
        out_shape=[
            jax.ShapeDtypeStruct((T_ALL, SSM_WIDTH), F32),
            jax.ShapeDtypeStruct((BATCH, N_SLABS * H), F32),
            jax.ShapeDtypeStruct((BATCH, N_SLABS * H), F32),
            jax.ShapeDtypeStruct((DEC_BATCH, N_SLABS * H), F32),
            jax.ShapeDtypeStruct((DEC_BATCH, N_SLABS * H), F32),
        ],
        scratch_shapes=_ssm_operator_scratch(CHUNK_P) + _ssm_operator_scratch(CHUNK_S),
        compiler_params=_cparams("parallel"),
        name="ssm",
    )(u_all, h0_r, h0_i, lam, logdt, bt_re, bt_im, c_re, c_im)


MIX_TILE = 640


def _mix_kernel(y_ref, u_ref, d_ref, wg_ref, gs_ref, a_ref, ga_ref, wo_ref, x_ref, o_ref):
    a_n = _rms(a_ref[...], ga_ref[...]).astype(BF16)
    mixed = jnp.dot(a_n, wo_ref[:ATTN_WIDTH, :], preferred_element_type=F32)
    y = y_ref[...] + d_ref[...] * u_ref[...]
    z = 0.5 * y * (1.0 + lax.erf(y * (1.0 / math.sqrt(2.0))))
    gate = jnp.dot(z.astype(BF16), wg_ref[...], preferred_element_type=F32)
    s_n = _rms(z * jax.nn.sigmoid(gate), gs_ref[...]).astype(BF16)
    mixed = mixed + jnp.dot(s_n, wo_ref[ATTN_WIDTH:, :], preferred_element_type=F32)
    o_ref[...] = x_ref[...] + mixed


def _mix(y_all, u_all, d, w_glu_b, g_ssm, a_all, g_attn, w_out_b, x_all, l):
    row = lambda i: (i, 0)
    par = lambda i: (l, 0, 0)
    whole = lambda shape: pl.BlockSpec(shape, lambda i: (0, 0), pipeline_mode=pl.Buffered(1))
    return pl.pallas_call(
        _mix_kernel,
        grid=(T_ALL // MIX_TILE,),
        in_specs=[
            pl.BlockSpec((MIX_TILE, SSM_WIDTH), row),
            pl.BlockSpec((MIX_TILE, SSM_WIDTH), row),
            pl.BlockSpec((None, 1, SSM_WIDTH), par),
            whole((SSM_WIDTH, SSM_WIDTH)),
            pl.BlockSpec((None, 1, SSM_WIDTH), par),
            pl.BlockSpec((MIX_TILE, ATTN_WIDTH), row),
            pl.BlockSpec((None, 1, ATTN_WIDTH), par),
            whole((D_MODEL, D_MODEL)),
            pl.BlockSpec((MIX_TILE, D_MODEL), row),
        ],
        out_specs=pl.BlockSpec((MIX_TILE, D_MODEL), row),
        out_shape=jax.ShapeDtypeStruct((T_ALL, D_MODEL), F32),
        compiler_params=_cparams("parallel"),
        name="mix",
    )(y_all, u_all, d, w_glu_b, g_ssm, a_all, g_attn, w_out_b, x_all)


def _ffn_body(x_ref, g_ref, wg_ref, wu_ref, wd_ref, o_ref, h_ref):
    @pl.when(pl.program_id(1) == 0)
    def _():
        x = x_ref[...]
        h_ref[...] = _rms(x, g_ref[...]).astype(BF16)
        o_ref[...] = x

    h = h_ref[...]
    gate = jnp.dot(h, wg_ref[...], preferred_element_type=F32)
    up = jnp.dot(h, wu_ref[...], preferred_element_type=F32)
    act = (gate * jax.nn.sigmoid(gate) * up).astype(BF16)
    o_ref[...] += jnp.dot(act, wd_ref[...], preferred_element_type=F32)


def _ffn_final_kernel(x_ref, g_ref, wg_ref, wu_ref, wd_ref, gf_ref, yp_ref, ys_ref, h_ref):
    _ffn_body(x_ref, g_ref, wg_ref, wu_ref, wd_ref, yp_ref, h_ref)

    @pl.when(pl.program_id(1) == _N_FF_TILES - 1)
    def _():
        y = _rms(yp_ref[...], gf_ref[...])
        yp_ref[...] = y

        @pl.when(pl.program_id(0) == _N_ROW_TILES - 1)
        def _():
            ys_ref[...] = y[ROW_TILE - T_SAMPLE:]


def _ffn_cast_kernel(x_ref, g_ref, wg_ref, wu_ref, wd_ref, ng_ref, nu_ref, nd_ref,
                     o_ref, ngb_ref, nub_ref, ndb_ref, h_ref):
    _ffn_body(x_ref, g_ref, wg_ref, wu_ref, wd_ref, o_ref, h_ref)
    ngb_ref[...] = ng_ref[...].astype(BF16)
    nub_ref[...] = nu_ref[...].astype(BF16)
    ndb_ref[...] = nd_ref[...].astype(BF16)


_N_ROW_TILES = T_ALL // ROW_TILE
_N_FF_TILES = D_FF // FF_TILE
_CAST_ROWS = D_MODEL // _N_ROW_TILES


def _ffn(x_all, g, w_gate_b, w_up_b, w_down_b, l, w_next=None, g_final=None):
    in_specs = [
        pl.BlockSpec((ROW_TILE, D_MODEL), lambda i, f: (i, 0)),
        pl.BlockSpec((None, 1, D_MODEL), lambda i, f: (l, 0, 0)),
        pl.BlockSpec((D_MODEL, FF_TILE), lambda i, f: (0, f)),
        pl.BlockSpec((D_MODEL, FF_TILE), lambda i, f: (0, f)),
        pl.BlockSpec((FF_TILE, D_MODEL), lambda i, f: (f, 0)),
    ]
    out_specs = [pl.BlockSpec((ROW_TILE, D_MODEL), lambda i, f: (i, 0))]
    out_shape = [jax.ShapeDtypeStruct((T_ALL, D_MODEL), F32)]
    args = [x_all, g, w_gate_b, w_up_b, w_down_b]
    if w_next is not None:
        assert D_MODEL % _N_ROW_TILES == 0
        in_specs += [
            pl.BlockSpec((None, _CAST_ROWS, FF_TILE), lambda i, f: (l + 1, i, f)),
            pl.BlockSpec((None, _CAST_ROWS, FF_TILE), lambda i, f: (l + 1, i, f)),
            pl.BlockSpec((None, FF_TILE, _CAST_ROWS), lambda i, f: (l + 1, f, i)),
        ]
        out_specs += [
            pl.BlockSpec((_CAST_ROWS, FF_TILE), lambda i, f: (i, f)),
            pl.BlockSpec((_CAST_ROWS, FF_TILE), lambda i, f: (i, f)),
            pl.BlockSpec((FF_TILE, _CAST_ROWS), lambda i, f: (f, i)),
        ]
        out_shape += [jax.ShapeDtypeStruct((D_MODEL, D_FF), BF16), jax.ShapeDtypeStruct((D_MODEL, D_FF), BF16),
                      jax.ShapeDtypeStruct((D_FF, D_MODEL), BF16)]
        args += list(w_next)
    else:
        assert T_PROMPT > (_N_ROW_TILES - 1) * ROW_TILE and T_ALL == _N_ROW_TILES * ROW_TILE
        in_specs.append(pl.BlockSpec((1, D_MODEL), lambda i, f: (0, 0)))
        out_specs = [pl.BlockSpec((ROW_TILE, D_MODEL), lambda i, f: (i, 0)),
                     pl.BlockSpec((T_SAMPLE, D_MODEL), lambda i, f: (0, 0))]
        out_shape = [jax.ShapeDtypeStruct((T_PROMPT, D_MODEL), F32), jax.ShapeDtypeStruct((T_SAMPLE, D_MODEL), F32)]
        args.append(g_final)
    return pl.pallas_call(
        _ffn_final_kernel if w_next is None else _ffn_cast_kernel,
        grid=(_N_ROW_TILES, _N_FF_TILES),
        in_specs=in_specs,
        out_specs=out_specs,
        out_shape=out_shape,
        scratch_shapes=[pltpu.VMEM((ROW_TILE, D_MODEL), BF16)],
        compiler_params=_cparams("parallel" if w_next is not None else "arbitrary", "arbitrary"),
        name="ffn",
    )(*args)


def kernel(x_prompt, x_sample, cache_k, cache_v, state_ssm_re, state_ssm_im, norm_mix, w_in, attn_sink, ssm_a_re,
           ssm_a_im, ssm_log_dt, ssm_b_re, ssm_b_im, ssm_c_re, ssm_c_im, ssm_d, w_glu, norm_attn_out, norm_ssm_out,
           w_out, norm_ffn, w_gate, w_up, w_down, norm_final):
    G, P, H = N_SSM_GROUPS, SSM_STATE, SLAB_STATE
    w_in_b, w_out_b, w_glu_b = w_in[0].astype(BF16), w_out[0].astype(BF16), w_glu[0].astype(BF16)
    row3 = lambda p: p.reshape(DEPTH, 1, -1)
    g_mix, g_attn, g_ssm, g_ffn, d3 = row3(norm_mix), row3(norm_attn_out), row3(norm_ssm_out), row3(norm_ffn), row3(ssm_d)
    lam = jnp.stack([ssm_a_re.reshape(DEPTH, N_SLABS, H), ssm_a_im.reshape(DEPTH, N_SLABS, H)], axis=2)
    logdt = jnp.repeat(ssm_log_dt, P, axis=1).reshape(DEPTH, N_SLABS, 1, H)
    bt_re, bt_im = ssm_b_re.transpose(0, 1, 3, 2), ssm_b_im.transpose(0, 1, 3, 2)
    cache_k2 = cache_k.reshape(DEPTH, DEC_BATCH, WINDOW, KV_WIDTH)
    cache_v2 = cache_v.reshape(DEPTH, DEC_BATCH, WINDOW, KV_WIDTH)

    k_p, v_p, hr_p, hi_p, k_s, v_s, hr_s, hi_s = [], [], [], [], [], [], [], []
    for l in range(DEPTH):
        if l == 0:
            q_all, kv_all, u_all, x_all = _in_proj_first(x_prompt.reshape(T_PROMPT, D_MODEL),
                                                         x_sample.reshape(T_SAMPLE, D_MODEL), g_mix, w_in_b)
        else:
            q_all, kv_all, u_all = _in_proj(x_all, g_mix, w_in_b, l)

        casts = []
        if l == 0:
            casts += [(w_gate, 0, D_MODEL // 64), (w_up, 0, D_MODEL // 64), (w_down, 0, D_FF // 32)]
        if l + 1 < DEPTH:
            casts += [(w_in, l + 1, D_MODEL // 64), (w_out, l + 1, D_MODEL // 64), (w_glu, l + 1, SSM_WIDTH // 64)]
        a_all, *cast_out = _attention(attn_sink[l], q_all, kv_all, cache_k2, cache_v2, l, casts)
        if l == 0:
            ffn_w, cast_out = cast_out[:3], cast_out[3:]
        w_glu_cur, w_out_cur = w_glu_b, w_out_b
        if l + 1 < DEPTH:
            w_in_b, w_out_b, w_glu_b = cast_out

        y_all, sfin_r, sfin_i, snew_r, snew_i = _ssm(
            u_all, state_ssm_re[l].reshape(DEC_BATCH, G * P), state_ssm_im[l].reshape(DEC_BATCH, G * P),
            lam, logdt, bt_re, bt_im, ssm_c_re, ssm_c_im, l)
        x_all = _mix(y_all, u_all, d3, w_glu_cur, g_ssm, a_all, g_attn, w_out_cur, x_all, l)
        if l + 1 < DEPTH:
            x_all, *ffn_w = _ffn(x_all, g_ffn, *ffn_w, l, w_next=(w_gate, w_up, w_down))
        else:
            y_prompt, y_sample = _ffn(x_all, g_ffn, *ffn_w, l, g_final=norm_final.reshape(1, D_MODEL))

        kv_p = jnp.stack([kv_all[(b + 1) * SEQ - WINDOW:(b + 1) * SEQ] for b in range(BATCH)])
        kv_p = kv_p.reshape(BATCH, WINDOW, 2, N_KV_HEADS, HEAD_DIM)
        k_p.append(kv_p[:, :, 0])
        v_p.append(kv_p[:, :, 1])
        kv_s = kv_all[T_PROMPT:].reshape(DEC_BATCH, DEC_SEQ, 2, N_KV_HEADS, HEAD_DIM)
        k_s.append(jnp.concatenate([cache_k[l][:, DEC_SEQ:], kv_s[:, :, 0]], axis=1))
        v_s.append(jnp.concatenate([cache_v[l][:, DEC_SEQ:], kv_s[:, :, 1]], axis=1))
        hr_p.append(sfin_r.reshape(BATCH, G, P))
        hi_p.append(sfin_i.reshape(BATCH, G, P))
        hr_s.append(snew_r.reshape(DEC_BATCH, G, P))
        hi_s.append(snew_i.reshape(DEC_BATCH, G, P))

    y_prompt = y_prompt.reshape(BATCH, SEQ, D_MODEL)
    y_sample = y_sample.reshape(DEC_BATCH, DEC_SEQ, D_MODEL)
    st = jnp.stack
    return (y_prompt, y_sample, st(k_p), st(v_p), st(hr_p), st(hi_p), st(k_s), st(v_s), st(hr_s), st(hi_s))
```

```python
import functools
import math

import jax
import jax.numpy as jnp
from jax import lax
from jax.experimental import pallas as pl
from jax.experimental.pallas import tpu as pltpu

F32 = jnp.float32
BF16 = jnp.bfloat16

D_MODEL = 2048
BATCH = 2
SEQ = 4096
DEPTH = 4
DEC_BATCH = 32
DEC_SEQ = 4
ATTN_WIDTH = 1024
SSM_WIDTH = 1024
HEAD_DIM = 64
N_HEADS = 16
N_KV_HEADS = 2
GQA_GROUP = 8
KV_WIDTH = 128
WINDOW = 128
SSM_GROUP = 16
N_SSM_GROUPS = 64
SSM_STATE = 64
D_IN = ATTN_WIDTH + 2 * KV_WIDTH + SSM_WIDTH
D_FF = 5632
EPS = 1e-5

T_PROMPT = BATCH * SEQ
T_SAMPLE = DEC_BATCH * DEC_SEQ
T_ALL = T_PROMPT + T_SAMPLE
ROW_TILE = 1040
FF_TILE = 512
CHUNK_P = 16
CHUNK_S = DEC_SEQ
SLAB = 128
GROUPS_PER_SLAB = SLAB // SSM_GROUP
N_SLABS = SSM_WIDTH // SLAB
SLAB_STATE = GROUPS_PER_SLAB * SSM_STATE
MASK_NEG = -1e30
V7X_VMEM_BYTES = 64 * 1024 * 1024
VMEM_LIMIT = V7X_VMEM_BYTES - 4 * 1024 * 1024

_TRANS_B = (((1,), (1,)), ((), ()))
_HI = lax.Precision.HIGHEST


def _cparams(*sem):
    return pltpu.CompilerParams(dimension_semantics=sem, vmem_limit_bytes=VMEM_LIMIT)


def _rms(x, g):
    ms = jnp.mean(x * x, axis=-1, keepdims=True)
    return x * lax.rsqrt(ms + EPS) * g


def _in_proj_kernel(x_ref, g_ref, w_ref, q_ref, kv_ref, u_ref):
    h = _rms(x_ref[...], g_ref[...]).astype(BF16)
    z = jnp.dot(h, w_ref[...], preferred_element_type=F32)
    q_ref[...] = (z[:, :ATTN_WIDTH] * (HEAD_DIM ** -0.5)).astype(BF16)
    kv_ref[...] = z[:, ATTN_WIDTH:ATTN_WIDTH + 2 * KV_WIDTH]
    u_ref[...] = z[:, ATTN_WIDTH + 2 * KV_WIDTH:]


def _in_proj_out(tile):
    row = lambda i: (i, 0)
    specs = [pl.BlockSpec((tile, ATTN_WIDTH), row), pl.BlockSpec((tile, 2 * KV_WIDTH), row),
             pl.BlockSpec((tile, SSM_WIDTH), row)]
    shapes = [jax.ShapeDtypeStruct((T_ALL, ATTN_WIDTH), BF16), jax.ShapeDtypeStruct((T_ALL, 2 * KV_WIDTH), F32),
              jax.ShapeDtypeStruct((T_ALL, SSM_WIDTH), F32)]
    return specs, shapes


def _in_proj(x_all, g, w_in_b, l):
    out_specs, out_shapes = _in_proj_out(ROW_TILE)
    return pl.pallas_call(
        _in_proj_kernel,
        grid=(T_ALL // ROW_TILE,),
        in_specs=[
            pl.BlockSpec((ROW_TILE, D_MODEL), lambda i: (i, 0)),
            pl.BlockSpec((None, 1, D_MODEL), lambda i: (l, 0, 0)),
            pl.BlockSpec((D_MODEL, D_IN), lambda i: (0, 0), pipeline_mode=pl.Buffered(1)),
        ],
        out_specs=out_specs,
        out_shape=out_shapes,
        compiler_params=_cparams("parallel"),
        name="in_proj",
    )(x_all, g, w_in_b)


FIRST_TILE = 640
_LAST_TILE = T_ALL // FIRST_TILE - 1
_PROMPT_ROWS_LAST_TILE = T_PROMPT - _LAST_TILE * FIRST_TILE


def _in_proj_first_kernel(xp_ref, xs_ref, g_ref, w_ref, q_ref, kv_ref, u_ref, x_ref):
    i = pl.program_id(0)

    @pl.when(i < _LAST_TILE)
    def _():
        x_ref[...] = xp_ref[...]

    @pl.when(i == _LAST_TILE)
    def _():
        x_ref[:_PROMPT_ROWS_LAST_TILE, :] = xp_ref[:_PROMPT_ROWS_LAST_TILE, :]
        x_ref[_PROMPT_ROWS_LAST_TILE:, :] = xs_ref[...]

    _in_proj_kernel(x_ref, g_ref, w_ref, q_ref, kv_ref, u_ref)


def _in_proj_first(x_prompt, x_sample, g, w_in_b):
    assert FIRST_TILE - _PROMPT_ROWS_LAST_TILE == T_SAMPLE
    out_specs, out_shapes = _in_proj_out(FIRST_TILE)
    return pl.pallas_call(
        _in_proj_first_kernel,
        grid=(T_ALL // FIRST_TILE,),
        in_specs=[
            pl.BlockSpec((FIRST_TILE, D_MODEL), lambda i: (i, 0)),
            pl.BlockSpec((T_SAMPLE, D_MODEL), lambda i: (0, 0)),
            pl.BlockSpec((None, 1, D_MODEL), lambda i: (0, 0, 0)),
            pl.BlockSpec((D_MODEL, D_IN), lambda i: (0, 0), pipeline_mode=pl.Buffered(1)),
        ],
        out_specs=out_specs + [pl.BlockSpec((FIRST_TILE, D_MODEL), lambda i: (i, 0))],
        out_shape=out_shapes + [jax.ShapeDtypeStruct((T_ALL, D_MODEL), F32)],
        compiler_params=_cparams("parallel"),
        name="in_proj_first",
    )(x_prompt, x_sample, g, w_in_b)


def _softmax_pv(s_parts, bias_parts, v_parts, sinks, rows):
    p_parts = [[] for _ in s_parts]
    dens = []
    for g in range(GQA_GROUP):
        sl = slice(g * rows, (g + 1) * rows)
        sk = sinks[g]
        sg = [s[sl] + b for s, b in zip(s_parts, bias_parts)]
        m = sk
        for s in sg:
            m = jnp.maximum(jnp.max(s, axis=-1, keepdims=True), m)
        den = jnp.exp(sk - m)
        for j, s in enumerate(sg):
            p = jnp.exp(s - m)
            den = den + jnp.sum(p, axis=-1, keepdims=True)
            p_parts[j].append(p.astype(BF16))
        dens.append(den)
    o = None
    for j, v in enumerate(v_parts):
        pj = jnp.concatenate(p_parts[j], axis=0)
        oj = jnp.dot(pj, v, preferred_element_type=F32)
        o = oj if o is None else o + oj
    return [o[g * rows:(g + 1) * rows] / dens[g] for g in range(GQA_GROUP)]


def _head_cols(h):
    return slice(h * HEAD_DIM, (h + 1) * HEAD_DIM)


def _stack_heads(q, hk):
    return jnp.concatenate([q[:, _head_cols(hk * GQA_GROUP + g)] for g in range(GQA_GROUP)], axis=0)


def _attn_prompt_block(n, sinks, q, kvc, kvp):
    row = lax.broadcasted_iota(jnp.int32, (WINDOW, 2 * WINDOW), 0)
    col = lax.broadcasted_iota(jnp.int32, (WINDOW, 2 * WINDOW), 1)
    ok = (col > row) & (col <= row + WINDOW) & ((col >= WINDOW) | (n > 0))
    bias = jnp.where(ok, 0.0, MASK_NEG).astype(F32)
    outs = []
    for hk in range(N_KV_HEADS):
        ks, vs = _head_cols(hk), _head_cols(N_KV_HEADS + hk)
        k = jnp.concatenate([kvp[:, ks], kvc[:, ks]], axis=0).astype(BF16)
        v = jnp.concatenate([kvp[:, vs], kvc[:, vs]], axis=0).astype(BF16)
        for g in range(GQA_GROUP):
            h = hk * GQA_GROUP + g
            s = lax.dot_general(q[:, _head_cols(h)], k, _TRANS_B, preferred_element_type=F32) + bias
            m = jnp.maximum(jnp.max(s, axis=-1, keepdims=True), sinks[h])
            p = jnp.exp(s - m)
            den = jnp.sum(p, axis=-1, keepdims=True) + jnp.exp(sinks[h] - m)
            outs.append(jnp.dot(p.astype(BF16), v, preferred_element_type=F32) / den)
    return jnp.concatenate(outs, axis=1)


_SEQ_PER_CHUNK = 4
_ROWS_S = _SEQ_PER_CHUNK * DEC_SEQ


def _attn_sample_chunk(sinks, q, kv, kc, vc):
    r = lax.broadcasted_iota(jnp.int32, (_ROWS_S, _SEQ_PER_CHUNK * WINDOW), 0)
    c = lax.broadcasted_iota(jnp.int32, (_ROWS_S, _SEQ_PER_CHUNK * WINDOW), 1)
    ok_c = (c // WINDOW == r // DEC_SEQ) & (c % WINDOW > r % DEC_SEQ)
    bias_c = jnp.where(ok_c, 0.0, MASK_NEG).astype(F32)
    r2 = lax.broadcasted_iota(jnp.int32, (_ROWS_S, _ROWS_S), 0)
    c2 = lax.broadcasted_iota(jnp.int32, (_ROWS_S, _ROWS_S), 1)
    ok_n = (c2 // DEC_SEQ == r2 // DEC_SEQ) & (c2 % DEC_SEQ <= r2 % DEC_SEQ)
    bias_n = jnp.where(ok_n, 0.0, MASK_NEG).astype(F32)
    outs = []
    for hk in range(N_KV_HEADS):
        ks, vs = _head_cols(hk), _head_cols(N_KV_HEADS + hk)
        k_c, v_c = kc[:, ks].astype(BF16), vc[:, ks].astype(BF16)
        k_n, v_n = kv[:, ks].astype(BF16), kv[:, vs].astype(BF16)
        qs = _stack_heads(q, hk)
        s_c = lax.dot_general(qs, k_c, _TRANS_B, preferred_element_type=F32)
        s_n = lax.dot_general(qs, k_n, _TRANS_B, preferred_element_type=F32)
        outs += _softmax_pv([s_c, s_n], [bias_c, bias_n], [v_c, v_n],
                            sinks[hk * GQA_GROUP:(hk + 1) * GQA_GROUP], _ROWS_S)
    return jnp.concatenate(outs, axis=1)


_N_PROMPT_BLOCKS = T_PROMPT // WINDOW


def _attn_kernel(n_cast, sink_ref, q_ref, kvc_ref, kvp_ref, kc_ref, vc_ref, *rest):
    o_ref = rest[n_cast]
    for src_ref, dst_ref in zip(rest[:n_cast], rest[n_cast + 1:]):
        dst_ref[...] = src_ref[...].astype(BF16)
    i = pl.program_id(0)
    sinks = [sink_ref[h] for h in range(N_HEADS)]

    @pl.when(i < _N_PROMPT_BLOCKS)
    def _():
        o_ref[...] = _attn_prompt_block(i % (SEQ // WINDOW), sinks, q_ref[...], kvc_ref[...], kvp_ref[...])

    @pl.when(i == _N_PROMPT_BLOCKS)
    def _():
        def chunk(c, carry):
            rows = pl.ds(pl.multiple_of(c * _ROWS_S, _ROWS_S), _ROWS_S)
            seqs = pl.ds(c * _SEQ_PER_CHUNK, _SEQ_PER_CHUNK)
            kc = kc_ref[seqs].reshape(_SEQ_PER_CHUNK * WINDOW, KV_WIDTH)
            vc = vc_ref[seqs].reshape(_SEQ_PER_CHUNK * WINDOW, KV_WIDTH)
            o_ref[rows, :] = _attn_sample_chunk(sinks, q_ref[rows, :], kvc_ref[rows, :], kc, vc)
            return carry
        lax.fori_loop(0, DEC_BATCH // _SEQ_PER_CHUNK, chunk, 0)


def _attention(sink, q_all, kv_all, kc, vc, l, casts=()):
    nb = SEQ // WINDOW
    cache = pl.BlockSpec((None, DEC_BATCH, WINDOW, KV_WIDTH), lambda i: (l, 0, 0, 0), pipeline_mode=pl.Buffered(1))
    in_specs = [
        pl.BlockSpec(memory_space=pltpu.SMEM),
        pl.BlockSpec((WINDOW, ATTN_WIDTH), lambda i: (i, 0)),
        pl.BlockSpec((WINDOW, 2 * KV_WIDTH), lambda i: (i, 0)),
        pl.BlockSpec((WINDOW, 2 * KV_WIDTH), lambda i: (jnp.where(i % nb == 0, i, i - 1), 0)),
        cache,
        cache,
    ]
    out_specs = [pl.BlockSpec((WINDOW, ATTN_WIDTH), lambda i: (i, 0))]
    out_shape = [jax.ShapeDtypeStruct((T_ALL, ATTN_WIDTH), F32)]
    for w, layer, rows in casts:
        _, n_rows, n_cols = w.shape
        n_blocks = n_rows // rows
        assert n_blocks * rows == n_rows and _N_PROMPT_BLOCKS % n_blocks == 0
        steps = _N_PROMPT_BLOCKS // n_blocks
        blk = lambda i, steps=steps, n_blocks=n_blocks: jnp.minimum(i // steps, n_blocks - 1)
        in_specs.append(pl.BlockSpec((None, rows, n_cols), lambda i, blk=blk, layer=layer: (layer, blk(i), 0)))
        out_specs.append(pl.BlockSpec((rows, n_cols), lambda i, blk=blk: (blk(i), 0)))
        out_shape.append(jax.ShapeDtypeStruct((n_rows, n_cols), BF16))
    return pl.pallas_call(
        functools.partial(_attn_kernel, len(casts)),
        grid=(_N_PROMPT_BLOCKS + 1,),
        in_specs=in_specs,
        out_specs=out_specs,
        out_shape=out_shape,
        compiler_params=_cparams("arbitrary"),
        name="attention",
    )(sink, q_all, kv_all, kv_all, kc, vc, *[w for w, _, _ in casts])


def _expand_groups(m):
    m2 = m.reshape(SLAB, SSM_STATE)
    t = jnp.concatenate([m2] * GROUPS_PER_SLAB, axis=1)
    r = lax.broadcasted_iota(jnp.int32, (SLAB, SLAB_STATE), 0) // SSM_GROUP
    c = lax.broadcasted_iota(jnp.int32, (SLAB, SLAB_STATE), 1) // SSM_STATE
    return jnp.where(r == c, t, 0.0)


def _fold_groups(x):
    t = [x[:, i * SLAB:(i + 1) * SLAB] for i in range(2 * SLAB_STATE // SLAB)]
    return jnp.concatenate([t[0] + t[1] + t[2] + t[3], t[4] + t[5] + t[6] + t[7]], axis=1)


def _ssm_tables(lam_ref, logdt_ref, bre_ref, bim_ref, cre_ref, cim_ref, max_chunk):
    H = SLAB_STATE
    dt = jnp.exp(logdt_ref[...])
    lam = lam_ref[...]
    lr, li = lam[0:1], lam[1:2]
    n_tau = ((max_chunk + 1 + 7) // 8) * 8
    tau = lax.broadcasted_iota(jnp.int32, (n_tau, H), 0).astype(F32)
    mag = jnp.exp(tau * (lr * dt))
    ang = tau * (li * dt)
    pw_r, pw_i = mag * jnp.cos(ang), mag * jnp.sin(ang)
    abr, abi = pw_r[1:2], pw_i[1:2]
    nr = abr - 1.0
    den = lr * lr + li * li
    cr = (nr * lr + abi * li) / den
    ci = (abi * lr - nr * li) / den
    xb_r, xb_i = _expand_groups(bre_ref[...]), _expand_groups(bim_ref[...])
    xc_r, xc_i = _expand_groups(cre_ref[...]), _expand_groups(cim_ref[...])
    return pw_r, pw_i, cr, ci, xb_r, xb_i, xc_r, xc_i


def _ssm_operators(L, tables, tt_ref, bst_ref, cst_ref):
    pw_r, pw_i, cr, ci, xb_r, xb_i, xc_r, xc_i = tables
    ccm = _fold_groups(jnp.concatenate([xc_r, -xc_i], axis=1))
    same_group = (lax.broadcasted_iota(jnp.int32, (SLAB, SLAB), 0) // SSM_GROUP
                  == lax.broadcasted_iota(jnp.int32, (SLAB, SLAB), 1) // SSM_GROUP)

    k_lag = [None] * L
    for s in range(L):
        e = L - 1 - s
        w_r = cr * pw_r[e:e + 1] - ci * pw_i[e:e + 1]
        w_i = cr * pw_i[e:e + 1] + ci * pw_r[e:e + 1]
        slab = jnp.concatenate([xb_r * w_r - xb_i * w_i, xb_r * w_i + xb_i * w_r], axis=1)
        bst_ref[s * SLAB:(s + 1) * SLAB, :] = slab.astype(BF16)
        k_all = lax.dot_general(_fold_groups(slab), ccm, _TRANS_B, preferred_element_type=F32, precision=_HI)
        k_lag[e] = jnp.where(same_group, k_all, 0.0)
    for s in range(L):
        for t_lo in range(2):
            lag = L - 2 + t_lo - s
            blk = k_lag[lag] if lag >= 0 else jnp.zeros((SLAB, SLAB), F32)
            tt_ref[s * SLAB:(s + 1) * SLAB, t_lo * SLAB:(t_lo + 1) * SLAB] = blk.astype(BF16)
    for t in range(L):
        p_r, p_i = pw_r[t + 1:t + 2], pw_i[t + 1:t + 2]
        g_t = jnp.concatenate([xc_r * p_r - xc_i * p_i, -(xc_r * p_i + xc_i * p_r)], axis=1)
        cst_ref[:, t * SLAB:(t + 1) * SLAB] = g_t.T.astype(BF16)
    return pw_r[L:L + 1], pw_i[L:L + 1]


_SCAN_ROWS = 8


def _scan_chunk_states(s_r, s_i, a_r, a_i, n_per_seq):
    n, H = s_r.shape
    j8 = lax.broadcasted_iota(jnp.int32, (n, H), 0) % _SCAN_ROWS
    p_r, p_i = a_r, a_i
    d = 1
    while d < _SCAN_ROWS:
        keep = j8 >= d
        sh_r = jnp.where(keep, pltpu.roll(s_r, d, axis=0), 0.0)
        sh_i = jnp.where(keep, pltpu.roll(s_i, d, axis=0), 0.0)
        s_r, s_i = s_r + sh_r * p_r - sh_i * p_i, s_i + sh_r * p_i + sh_i * p_r
        p_r, p_i = p_r * p_r - p_i * p_i, 2.0 * p_r * p_i
        d *= 2
    rows_r, rows_i = [a_r], [a_i]
    for _ in range(_SCAN_ROWS - 1):
        q_r, q_i = rows_r[-1], rows_i[-1]
        rows_r.append(q_r * a_r - q_i * a_i)
        rows_i.append(q_r * a_i + q_i * a_r)
    t_r, t_i = jnp.concatenate(rows_r, axis=0), jnp.concatenate(rows_i, axis=0)
    out_r, out_i = [], []
    for g in range(n // _SCAN_ROWS):
        b_r = s_r[g * _SCAN_ROWS:(g + 1) * _SCAN_ROWS]
        b_i = s_i[g * _SCAN_ROWS:(g + 1) * _SCAN_ROWS]
        if (g * _SCAN_ROWS) % n_per_seq != 0:
            c_r, c_i = out_r[-1][_SCAN_ROWS - 1:], out_i[-1][_SCAN_ROWS - 1:]
            b_r, b_i = b_r + t_r * c_r - t_i * c_i, b_i + t_r * c_i + t_i * c_r
        out_r.append(b_r)
        out_i.append(b_i)
    return jnp.concatenate(out_r, axis=0), jnp.concatenate(out_i, axis=0)


def _ssm_rows(u_ref, y_ref, row0, n_rows, L, ops, n_per_seq=None, h0=None):
    tt_ref, bst_ref, cst_ref, (a_r, a_i) = ops
    H = SLAB_STATE
    nj = n_rows // L
    lhs = jnp.concatenate([u_ref[pl.ds(row0 + s, nj, stride=L), :].astype(BF16) for s in range(L)], axis=1)
    e = jnp.dot(lhs, bst_ref[...], preferred_element_type=F32)
    s_r, s_i = e[:, :H], e[:, H:]
    if h0 is not None:
        in_r, in_i = h0
        s_r = s_r + in_r * a_r - in_i * a_i
        s_i = s_i + in_r * a_i + in_i * a_r
    else:
        s_r, s_i = _scan_chunk_states(s_r, s_i, a_r, a_i, n_per_seq)
        jj = lax.broadcasted_iota(jnp.int32, (nj, H), 0) % n_per_seq
        in_r = jnp.where(jj >= 1, pltpu.roll(s_r, 1, axis=0), 0.0)
        in_i = jnp.where(jj >= 1, pltpu.roll(s_i, 1, axis=0), 0.0)
    y_st = jnp.dot(jnp.concatenate([in_r, in_i], axis=1).astype(BF16), cst_ref[...], preferred_element_type=F32)
    for tp in range(L // 2):
        k = 2 * SLAB * (tp + 1)
        y = jnp.dot(lhs[:, :k], tt_ref[2 * SLAB * (L // 2 - 1 - tp):, :], preferred_element_type=F32)
        y = y + y_st[:, 2 * SLAB * tp:2 * SLAB * (tp + 1)]
        y_ref[pl.ds(row0 + 2 * tp, nj, stride=L), :] = y[:, :SLAB]
        y_ref[pl.ds(row0 + 2 * tp + 1, nj, stride=L), :] = y[:, SLAB:]
    return s_r, s_i


def _ssm_kernel(u_ref, h0r_ref, h0i_ref, lam_ref, logdt_ref, bre_ref, bim_ref, cre_ref, cim_ref,
                y_ref, pr_ref, pi_ref, sr_ref, si_ref,
                tt_p, bst_p, cst_p, tt_s, bst_s, cst_s):
    tables = _ssm_tables(lam_ref, logdt_ref, bre_ref, bim_ref, cre_ref, cim_ref, max(CHUNK_P, CHUNK_S))
    al_p = _ssm_operators(CHUNK_P, tables, tt_p, bst_p, cst_p)
    al_s = _ssm_operators(CHUNK_S, tables, tt_s, bst_s, cst_s)
    n_per_seq = SEQ // CHUNK_P
    s_r, s_i = _ssm_rows(u_ref, y_ref, 0, T_PROMPT, CHUNK_P, (tt_p, bst_p, cst_p, al_p), n_per_seq=n_per_seq)
    for b in range(BATCH):
        last = (b + 1) * n_per_seq - 1
        pr_ref[b:b + 1, :] = s_r[last:last + 1]
        pi_ref[b:b + 1, :] = s_i[last:last + 1]
    n_r, n_i = _ssm_rows(u_ref, y_ref, T_PROMPT, T_SAMPLE, CHUNK_S, (tt_s, bst_s, cst_s, al_s),
                         h0=(h0r_ref[...], h0i_ref[...]))
    sr_ref[...] = n_r
    si_ref[...] = n_i


def _ssm_operator_scratch(L):
    H = SLAB_STATE
    return [pltpu.VMEM((L * SLAB, 2 * SLAB), BF16), pltpu.VMEM((L * SLAB, 2 * H), BF16),
            pltpu.VMEM((2 * H, L * SLAB), BF16)]


def _ssm(u_all, h0_r, h0_i, lam, logdt, bt_re, bt_im, c_re, c_im, l):
    H = SLAB_STATE
    slab = lambda rows: pl.BlockSpec((rows, SLAB), lambda q: (0, q))
    state = lambda rows: pl.BlockSpec((rows, H), lambda q: (0, q))
    state_in = pl.BlockSpec((None, DEC_BATCH, H), lambda q: (l, 0, q))
    q3 = lambda *blk: pl.BlockSpec((None, None) + blk, lambda q: (l, q, 0, 0))
    grp = pl.BlockSpec((None, GROUPS_PER_SLAB, SSM_GROUP, SSM_STATE), lambda q: (l, q, 0, 0))
    return pl.pallas_call(
        _ssm_kernel,
        grid=(N_SLABS,),
        in_specs=[slab(T_ALL), state_in, state_in, q3(2, H), q3(1, H), grp, grp, grp, grp],
        out_specs=[slab(T_ALL), state(BATCH), state(BATCH), state(DEC_BATCH), state(DEC_BATCH)],
        out_shape=[
            jax.ShapeDtypeStruct((T_ALL, SSM_WIDTH), F32),
            jax.ShapeDtypeStruct((BATCH, N_SLABS * H), F32),
            jax.ShapeDtypeStruct((BATCH, N_SLABS * H), F32),
            jax.ShapeDtypeStruct((DEC_BATCH, N_SLABS * H), F32),
            jax.ShapeDtypeStruct((DEC_BATCH, N_SLABS * H), F32),
        ],
        scratch_shapes=_ssm_operator_scratch(CHUNK_P) + _ssm_operator_scratch(CHUNK_S),
        compiler_params=_cparams("parallel"),
        name="ssm",
    )(u_all, h0_r, h0_i, lam, logdt, bt_re, bt_im, c_re, c_im)


MIX_TILE = 640


def _mix_kernel(y_ref, u_ref, d_ref, wg_ref, gs_ref, a_ref, ga_ref, wo_ref, x_ref, o_ref):
    a_n = _rms(a_ref[...], ga_ref[...]).astype(BF16)
    mixed = jnp.dot(a_n, wo_ref[:ATTN_WIDTH, :], preferred_element_type=F32)
    y = y_ref[...] + d_ref[...] * u_ref[...]
    z = 0.5 * y * (1.0 + lax.erf(y * (1.0 / math.sqrt(2.0))))
    gate = jnp.dot(z.astype(BF16), wg_ref[...], preferred_element_type=F32)
    s_n = _rms(z * jax.nn.sigmoid(gate), gs_ref[...]).astype(BF16)
    mixed = mixed + jnp.dot(s_n, wo_ref[ATTN_WIDTH:, :], preferred_element_type=F32)
    o_ref[...] = x_ref[...] + mixed


def _mix(y_all, u_all, d, w_glu_b, g_ssm, a_all, g_attn, w_out_b, x_all, l):
    row = lambda i: (i, 0)
    par = lambda i: (l, 0, 0)
    whole = lambda shape: pl.BlockSpec(shape, lambda i: (0, 0), pipeline_mode=pl.Buffered(1))
    return pl.pallas_call(
        _mix_kernel,
        grid=(T_ALL // MIX_TILE,),
        in_specs=[
            pl.BlockSpec((MIX_TILE, SSM_WIDTH), row),
            pl.BlockSpec((MIX_TILE, SSM_WIDTH), row),
            pl.BlockSpec((None, 1, SSM_WIDTH), par),
            whole((SSM_WIDTH, SSM_WIDTH)),
            pl.BlockSpec((None, 1, SSM_WIDTH), par),
            pl.BlockSpec((MIX_TILE, ATTN_WIDTH), row),
            pl.BlockSpec((None, 1, ATTN_WIDTH), par),
            whole((D_MODEL, D_MODEL)),
            pl.BlockSpec((MIX_TILE, D_MODEL), row),
        ],
        out_specs=pl.BlockSpec((MIX_TILE, D_MODEL), row),
        out_shape=jax.ShapeDtypeStruct((T_ALL, D_MODEL), F32),
        compiler_params=_cparams("parallel"),
        name="mix",
    )(y_all, u_all, d, w_glu_b, g_ssm, a_all, g_attn, w_out_b, x_all)


def _ffn_body(x_ref, g_ref, wg_ref, wu_ref, wd_ref, o_ref, h_ref):
    @pl.when(pl.program_id(1) == 0)
    def _():
        x = x_ref[...]
        h_ref[...] = _rms(x, g_ref[...]).astype(BF16)
        o_ref[...] = x

    h = h_ref[...]
    gate = jnp.dot(h, wg_ref[...], preferred_element_type=F32)
    up = jnp.dot(h, wu_ref[...], preferred_element_type=F32)
    act = (gate * jax.nn.sigmoid(gate) * up).astype(BF16)
    o_ref[...] += jnp.dot(act, wd_ref[...], preferred_element_type=F32)


def _ffn_final_kernel(x_ref, g_ref, wg_ref, wu_ref, wd_ref, gf_ref, yp_ref, ys_ref, h_ref):
    _ffn_body(x_ref, g_ref, wg_ref, wu_ref, wd_ref, yp_ref, h_ref)

    @pl.when(pl.program_id(1) == _N_FF_TILES - 1)
    def _():
        y = _rms(yp_ref[...], gf_ref[...])
        yp_ref[...] = y

        @pl.when(pl.program_id(0) == _N_ROW_TILES - 1)
        def _():
            ys_ref[...] = y[ROW_TILE - T_SAMPLE:]


def _ffn_cast_kernel(x_ref, g_ref, wg_ref, wu_ref, wd_ref, ng_ref, nu_ref, nd_ref,
                     o_ref, ngb_ref, nub_ref, ndb_ref, h_ref):
    _ffn_body(x_ref, g_ref, wg_ref, wu_ref, wd_ref, o_ref, h_ref)
    ngb_ref[...] = ng_ref[...].astype(BF16)
    nub_ref[...] = nu_ref[...].astype(BF16)
    ndb_ref[...] = nd_ref[...].astype(BF16)


_N_ROW_TILES = T_ALL // ROW_TILE
_N_FF_TILES = D_FF // FF_TILE
_CAST_ROWS = D_MODEL // _N_ROW_TILES


def _ffn(x_all, g, w_gate_b, w_up_b, w_down_b, l, w_next=None, g_final=None):
    in_specs = [
        pl.BlockSpec((ROW_TILE, D_MODEL), lambda i, f: (i, 0)),
        pl.BlockSpec((None, 1, D_MODEL), lambda i, f: (l, 0, 0)),
        pl.BlockSpec((D_MODEL, FF_TILE), lambda i, f: (0, f)),
        pl.BlockSpec((D_MODEL, FF_TILE), lambda i, f: (0, f)),
        pl.BlockSpec((FF_TILE, D_MODEL), lambda i, f: (f, 0)),
    ]
    out_specs = [pl.BlockSpec((ROW_TILE, D_MODEL), lambda i, f: (i, 0))]
    out_shape = [jax.ShapeDtypeStruct((T_ALL, D_MODEL), F32)]
    args = [x_all, g, w_gate_b, w_up_b, w_down_b]
    if w_next is not None:
        assert D_MODEL % _N_ROW_TILES == 0
        in_specs += [
            pl.BlockSpec((None, _CAST_ROWS, FF_TILE), lambda i, f: (l + 1, i, f)),
            pl.BlockSpec((None, _CAST_ROWS, FF_TILE), lambda i, f: (l + 1, i, f)),
            pl.BlockSpec((None, FF_TILE, _CAST_ROWS), lambda i, f: (l + 1, f, i)),
        ]
        out_specs += [
            pl.BlockSpec((_CAST_ROWS, FF_TILE), lambda i, f: (i, f)),
            pl.BlockSpec((_CAST_ROWS, FF_TILE), lambda i, f: (i, f)),
            pl.BlockSpec((FF_TILE, _CAST_ROWS), lambda i, f: (f, i)),
        ]
        out_shape += [jax.ShapeDtypeStruct((D_MODEL, D_FF), BF16), jax.ShapeDtypeStruct((D_MODEL, D_FF), BF16),
                      jax.ShapeDtypeStruct((D_FF, D_MODEL), BF16)]
        args += list(w_next)
    else:
        assert T_PROMPT > (_N_ROW_TILES - 1) * ROW_TILE and T_ALL == _N_ROW_TILES * ROW_TILE
        in_specs.append(pl.BlockSpec((1, D_MODEL), lambda i, f: (0, 0)))
        out_specs = [pl.BlockSpec((ROW_TILE, D_MODEL), lambda i, f: (i, 0)),
                     pl.BlockSpec((T_SAMPLE, D_MODEL), lambda i, f: (0, 0))]
        out_shape = [jax.ShapeDtypeStruct((T_PROMPT, D_MODEL), F32), jax.ShapeDtypeStruct((T_SAMPLE, D_MODEL), F32)]
        args.append(g_final)
    return pl.pallas_call(
        _ffn_final_kernel if w_next is None else _ffn_cast_kernel,
        grid=(_N_ROW_TILES, _N_FF_TILES),
        in_specs=in_specs,
        out_specs=out_specs,
        out_shape=out_shape,
        scratch_shapes=[pltpu.VMEM((ROW_TILE, D_MODEL), BF16)],
        compiler_params=_cparams("parallel" if w_next is not None else "arbitrary", "arbitrary"),
        name="ffn",
    )(*args)


def kernel(x_prompt, x_sample, cache_k, cache_v, state_ssm_re, state_ssm_im, norm_mix, w_in, attn_sink, ssm_a_re,
           ssm_a_im, ssm_log_dt, ssm_b_re, ssm_b_im, ssm_c_re, ssm_c_im, ssm_d, w_glu, norm_attn_out, norm_ssm_out,
           w_out, norm_ffn, w_gate, w_up, w_down, norm_final):
    G, P, H = N_SSM_GROUPS, SSM_STATE, SLAB_STATE
    w_in_b, w_out_b, w_glu_b = w_in[0].astype(BF16), w_out[0].astype(BF16), w_glu[0].astype(BF16)
    row3 = lambda p: p.reshape(DEPTH, 1, -1)
    g_mix, g_attn, g_ssm, g_ffn, d3 = row3(norm_mix), row3(norm_attn_out), row3(norm_ssm_out), row3(norm_ffn), row3(ssm_d)
    lam = jnp.stack([ssm_a_re.reshape(DEPTH, N_SLABS, H), ssm_a_im.reshape(DEPTH, N_SLABS, H)], axis=2)
    logdt = jnp.repeat(ssm_log_dt, P, axis=1).reshape(DEPTH, N_SLABS, 1, H)
    bt_re, bt_im = ssm_b_re.transpose(0, 1, 3, 2), ssm_b_im.transpose(0, 1, 3, 2)
    cache_k2 = cache_k.reshape(DEPTH, DEC_BATCH, WINDOW, KV_WIDTH)
    cache_v2 = cache_v.reshape(DEPTH, DEC_BATCH, WINDOW, KV_WIDTH)

    h0_r = state_ssm_re.reshape(DEPTH, DEC_BATCH, G * P)
    h0_i = state_ssm_im.reshape(DEPTH, DEC_BATCH, G * P)

    kv_p, kv_s, hr_p, hi_p, hr_s, hi_s = [], [], [], [], [], []
    for l in range(DEPTH):
        if l == 0:
            q_all, kv_all, u_all, x_all = _in_proj_first(x_prompt.reshape(T_PROMPT, D_MODEL),
                                                         x_sample.reshape(T_SAMPLE, D_MODEL), g_mix, w_in_b)
        else:
            q_all, kv_all, u_all = _in_proj(x_all, g_mix, w_in_b, l)

        casts = []
        if l == 0:
            casts += [(w_gate, 0, D_MODEL // 64), (w_up, 0, D_MODEL // 64), (w_down, 0, D_FF // 32)]
        if l + 1 < DEPTH:
            casts += [(w_in, l + 1, D_MODEL // 64), (w_out, l + 1, D_MODEL // 64), (w_glu, l + 1, SSM_WIDTH // 64)]
        a_all, *cast_out = _attention(attn_sink[l], q_all, kv_all, cache_k2, cache_v2, l, casts)
        if l == 0:
            ffn_w, cast_out = cast_out[:3], cast_out[3:]
        w_glu_cur, w_out_cur = w_glu_b, w_out_b
        if l + 1 < DEPTH:
            w_in_b, w_out_b, w_glu_b = cast_out

        y_all, sfin_r, sfin_i, snew_r, snew_i = _ssm(u_all, h0_r, h0_i, lam, logdt, bt_re, bt_im, ssm_c_re, ssm_c_im, l)
        x_all = _mix(y_all, u_all, d3, w_glu_cur, g_ssm, a_all, g_attn, w_out_cur, x_all, l)
        if l + 1 < DEPTH:
            x_all, *ffn_w = _ffn(x_all, g_ffn, *ffn_w, l, w_next=(w_gate, w_up, w_down))
        else:
            y_prompt, y_sample = _ffn(x_all, g_ffn, *ffn_w, l, g_final=norm_final.reshape(1, D_MODEL))

        kv_p += [kv_all[(b + 1) * SEQ - WINDOW:(b + 1) * SEQ] for b in range(BATCH)]
        kv_s.append(kv_all[T_PROMPT:])
        hr_p.append(sfin_r)
        hi_p.append(sfin_i)
        hr_s.append(snew_r)
        hi_s.append(snew_i)

    y_prompt = y_prompt.reshape(BATCH, SEQ, D_MODEL)
    y_sample = y_sample.reshape(DEC_BATCH, DEC_SEQ, D_MODEL)
    kv_p = jnp.stack(kv_p).reshape(DEPTH, BATCH, WINDOW, 2, N_KV_HEADS, HEAD_DIM)
    kv_s = jnp.stack(kv_s).reshape(DEPTH, DEC_BATCH, DEC_SEQ, 2, N_KV_HEADS, HEAD_DIM)
    k_s = jnp.concatenate([cache_k[:, :, DEC_SEQ:], kv_s[:, :, :, 0]], axis=2)
    v_s = jnp.concatenate([cache_v[:, :, DEC_SEQ:], kv_s[:, :, :, 1]], axis=2)
    states = lambda parts, batch: jnp.stack(parts).reshape(DEPTH, batch, G, P)
    return (y_prompt, y_sample, kv_p[:, :, :, 0], kv_p[:, :, :, 1], states(hr_p, BATCH), states(hi_p, BATCH),
            k_s, v_s, states(hr_s, DEC_BATCH), states(hi_s, DEC_BATCH))
```

```python
import functools
import math

import jax
import jax.numpy as jnp
from jax import lax
from jax.experimental import pallas as pl
from jax.experimental.pallas import tpu as pltpu

F32 = jnp.float32
BF16 = jnp.bfloat16

D_MODEL = 2048
BATCH = 2
SEQ = 4096
DEPTH = 4
DEC_BATCH = 32
DEC_SEQ = 4
ATTN_WIDTH = 1024
SSM_WIDTH = 1024
HEAD_DIM = 64
N_HEADS = 16
N_KV_HEADS = 2
GQA_GROUP = 8
KV_WIDTH = 128
WINDOW = 128
SSM_GROUP = 16
N_SSM_GROUPS = 64
SSM_STATE = 64
D_IN = ATTN_WIDTH + 2 * KV_WIDTH + SSM_WIDTH
D_FF = 5632
EPS = 1e-5

T_PROMPT = BATCH * SEQ
T_SAMPLE = DEC_BATCH * DEC_SEQ
T_ALL = T_PROMPT + T_SAMPLE
ROW_TILE = 1040
FF_TILE = 512
CHUNK_P = 8
CHUNK_S = DEC_SEQ
SLAB = 128
GROUPS_PER_SLAB = SLAB // SSM_GROUP
N_SLABS = SSM_WIDTH // SLAB
SLAB_STATE = GROUPS_PER_SLAB * SSM_STATE
MASK_NEG = -1e30
V7X_VMEM_BYTES = 64 * 1024 * 1024
VMEM_LIMIT = V7X_VMEM_BYTES - 4 * 1024 * 1024

_TRANS_B = (((1,), (1,)), ((), ()))
_HI = lax.Precision.HIGHEST


def _cparams(*sem):
    return pltpu.CompilerParams(dimension_semantics=sem, vmem_limit_bytes=VMEM_LIMIT)


def _rms(x, g):
    ms = jnp.mean(x * x, axis=-1, keepdims=True)
    return x * lax.rsqrt(ms + EPS) * g


def _in_proj_kernel(x_ref, g_ref, w_ref, q_ref, kv_ref, u_ref):
    h = _rms(x_ref[...], g_ref[...]).astype(BF16)
    z = jnp.dot(h, w_ref[...], preferred_element_type=F32)
    q_ref[...] = (z[:, :ATTN_WIDTH] * (HEAD_DIM ** -0.5)).astype(BF16)
    kv_ref[...] = z[:, ATTN_WIDTH:ATTN_WIDTH + 2 * KV_WIDTH]
    u_ref[...] = z[:, ATTN_WIDTH + 2 * KV_WIDTH:]


def _in_proj_out(tile):
    row = lambda i: (i, 0)
    specs = [pl.BlockSpec((tile, ATTN_WIDTH), row), pl.BlockSpec((tile, 2 * KV_WIDTH), row),
             pl.BlockSpec((tile, SSM_WIDTH), row)]
    shapes = [jax.ShapeDtypeStruct((T_ALL, ATTN_WIDTH), BF16), jax.ShapeDtypeStruct((T_ALL, 2 * KV_WIDTH), F32),
              jax.ShapeDtypeStruct((T_ALL, SSM_WIDTH), F32)]
    return specs, shapes


def _in_proj(x_all, g, w_in_b, l):
    out_specs, out_shapes = _in_proj_out(ROW_TILE)
    return pl.pallas_call(
        _in_proj_kernel,
        grid=(T_ALL // ROW_TILE,),
        in_specs=[
            pl.BlockSpec((ROW_TILE, D_MODEL), lambda i: (i, 0)),
            pl.BlockSpec((None, 1, D_MODEL), lambda i: (l, 0, 0)),
            pl.BlockSpec((D_MODEL, D_IN), lambda i: (0, 0), pipeline_mode=pl.Buffered(1)),
        ],
        out_specs=out_specs,
        out_shape=out_shapes,
        compiler_params=_cparams("parallel"),
        name="in_proj",
    )(x_all, g, w_in_b)


FIRST_TILE = 640
_LAST_TILE = T_ALL // FIRST_TILE - 1
_PROMPT_ROWS_LAST_TILE = T_PROMPT - _LAST_TILE * FIRST_TILE


def _in_proj_first_kernel(xp_ref, xs_ref, g_ref, w_ref, q_ref, kv_ref, u_ref, x_ref):
    i = pl.program_id(0)

    @pl.when(i < _LAST_TILE)
    def _():
        x_ref[...] = xp_ref[...]

    @pl.when(i == _LAST_TILE)
    def _():
        x_ref[:_PROMPT_ROWS_LAST_TILE, :] = xp_ref[:_PROMPT_ROWS_LAST_TILE, :]
        x_ref[_PROMPT_ROWS_LAST_TILE:, :] = xs_ref[...]

    _in_proj_kernel(x_ref, g_ref, w_ref, q_ref, kv_ref, u_ref)


def _in_proj_first(x_prompt, x_sample, g, w_in_b):
    assert FIRST_TILE - _PROMPT_ROWS_LAST_TILE == T_SAMPLE
    out_specs, out_shapes = _in_proj_out(FIRST_TILE)
    return pl.pallas_call(
        _in_proj_first_kernel,
        grid=(T_ALL // FIRST_TILE,),
        in_specs=[
            pl.BlockSpec((FIRST_TILE, D_MODEL), lambda i: (i, 0)),
            pl.BlockSpec((T_SAMPLE, D_MODEL), lambda i: (0, 0)),
            pl.BlockSpec((None, 1, D_MODEL), lambda i: (0, 0, 0)),
            pl.BlockSpec((D_MODEL, D_IN), lambda i: (0, 0), pipeline_mode=pl.Buffered(1)),
        ],
        out_specs=out_specs + [pl.BlockSpec((FIRST_TILE, D_MODEL), lambda i: (i, 0))],
        out_shape=out_shapes + [jax.ShapeDtypeStruct((T_ALL, D_MODEL), F32)],
        compiler_params=_cparams("parallel"),
        name="in_proj_first",
    )(x_prompt, x_sample, g, w_in_b)


def _softmax_pv(s_parts, bias_parts, v_parts, sinks, rows):
    p_parts = [[] for _ in s_parts]
    dens = []
    for g in range(GQA_GROUP):
        sl = slice(g * rows, (g + 1) * rows)
        sk = sinks[g]
        sg = [s[sl] + b for s, b in zip(s_parts, bias_parts)]
        m = sk
        for s in sg:
            m = jnp.maximum(jnp.max(s, axis=-1, keepdims=True), m)
        den = jnp.exp(sk - m)
        for j, s in enumerate(sg):
            p = jnp.exp(s - m)
            den = den + jnp.sum(p, axis=-1, keepdims=True)
            p_parts[j].append(p.astype(BF16))
        dens.append(den)
    o = None
    for j, v in enumerate(v_parts):
        pj = jnp.concatenate(p_parts[j], axis=0)
        oj = jnp.dot(pj, v, preferred_element_type=F32)
        o = oj if o is None else o + oj
    return [o[g * rows:(g + 1) * rows] / dens[g] for g in range(GQA_GROUP)]


def _head_cols(h):
    return slice(h * HEAD_DIM, (h + 1) * HEAD_DIM)


def _stack_heads(q, hk):
    return jnp.concatenate([q[:, _head_cols(hk * GQA_GROUP + g)] for g in range(GQA_GROUP)], axis=0)


def _attn_prompt_block(n, sinks, q, kvc, kvp):
    row = lax.broadcasted_iota(jnp.int32, (WINDOW, 2 * WINDOW), 0)
    col = lax.broadcasted_iota(jnp.int32, (WINDOW, 2 * WINDOW), 1)
    ok = (col > row) & (col <= row + WINDOW) & ((col >= WINDOW) | (n > 0))
    bias = jnp.where(ok, 0.0, MASK_NEG).astype(F32)
    outs = []
    for hk in range(N_KV_HEADS):
        ks, vs = _head_cols(hk), _head_cols(N_KV_HEADS + hk)
        k = jnp.concatenate([kvp[:, ks], kvc[:, ks]], axis=0).astype(BF16)
        v = jnp.concatenate([kvp[:, vs], kvc[:, vs]], axis=0).astype(BF16)
        for g in range(GQA_GROUP):
            h = hk * GQA_GROUP + g
            s = lax.dot_general(q[:, _head_cols(h)], k, _TRANS_B, preferred_element_type=F32) + bias
            m = jnp.maximum(jnp.max(s, axis=-1, keepdims=True), sinks[h])
            p = jnp.exp(s - m)
            den = jnp.sum(p, axis=-1, keepdims=True) + jnp.exp(sinks[h] - m)
            outs.append(jnp.dot(p.astype(BF16), v, preferred_element_type=F32) / den)
    return jnp.concatenate(outs, axis=1)


_SEQ_PER_CHUNK = 4
_ROWS_S = _SEQ_PER_CHUNK * DEC_SEQ


def _attn_sample_chunk(sinks, q, kv, kc, vc):
    r = lax.broadcasted_iota(jnp.int32, (_ROWS_S, _SEQ_PER_CHUNK * WINDOW), 0)
    c = lax.broadcasted_iota(jnp.int32, (_ROWS_S, _SEQ_PER_CHUNK * WINDOW), 1)
    ok_c = (c // WINDOW == r // DEC_SEQ) & (c % WINDOW > r % DEC_SEQ)
    bias_c = jnp.where(ok_c, 0.0, MASK_NEG).astype(F32)
    r2 = lax.broadcasted_iota(jnp.int32, (_ROWS_S, _ROWS_S), 0)
    c2 = lax.broadcasted_iota(jnp.int32, (_ROWS_S, _ROWS_S), 1)
    ok_n = (c2 // DEC_SEQ == r2 // DEC_SEQ) & (c2 % DEC_SEQ <= r2 % DEC_SEQ)
    bias_n = jnp.where(ok_n, 0.0, MASK_NEG).astype(F32)
    outs = []
    for hk in range(N_KV_HEADS):
        ks, vs = _head_cols(hk), _head_cols(N_KV_HEADS + hk)
        k_c, v_c = kc[:, ks].astype(BF16), vc[:, ks].astype(BF16)
        k_n, v_n = kv[:, ks].astype(BF16), kv[:, vs].astype(BF16)
        qs = _stack_heads(q, hk)
        s_c = lax.dot_general(qs, k_c, _TRANS_B, preferred_element_type=F32)
        s_n = lax.dot_general(qs, k_n, _TRANS_B, preferred_element_type=F32)
        outs += _softmax_pv([s_c, s_n], [bias_c, bias_n], [v_c, v_n],
                            sinks[hk * GQA_GROUP:(hk + 1) * GQA_GROUP], _ROWS_S)
    return jnp.concatenate(outs, axis=1)


_N_PROMPT_BLOCKS = T_PROMPT // WINDOW


def _attn_kernel(n_cast, sink_ref, q_ref, kvc_ref, kvp_ref, kc_ref, vc_ref, *rest):
    o_ref = rest[n_cast]
    for src_ref, dst_ref in zip(rest[:n_cast], rest[n_cast + 1:]):
        dst_ref[...] = src_ref[...].astype(BF16)
    i = pl.program_id(0)
    sinks = [sink_ref[h] for h in range(N_HEADS)]

    @pl.when(i < _N_PROMPT_BLOCKS)
    def _():
        o_ref[...] = _attn_prompt_block(i % (SEQ // WINDOW), sinks, q_ref[...], kvc_ref[...], kvp_ref[...])

    @pl.when(i == _N_PROMPT_BLOCKS)
    def _():
        def chunk(c, carry):
            rows = pl.ds(pl.multiple_of(c * _ROWS_S, _ROWS_S), _ROWS_S)
            seqs = pl.ds(c * _SEQ_PER_CHUNK, _SEQ_PER_CHUNK)
            kc = kc_ref[seqs].reshape(_SEQ_PER_CHUNK * WINDOW, KV_WIDTH)
            vc = vc_ref[seqs].reshape(_SEQ_PER_CHUNK * WINDOW, KV_WIDTH)
            o_ref[rows, :] = _attn_sample_chunk(sinks, q_ref[rows, :], kvc_ref[rows, :], kc, vc)
            return carry
        lax.fori_loop(0, DEC_BATCH // _SEQ_PER_CHUNK, chunk, 0)


def _attention(sink, q_all, kv_all, kc, vc, l, casts=()):
    nb = SEQ // WINDOW
    cache = pl.BlockSpec((None, DEC_BATCH, WINDOW, KV_WIDTH), lambda i: (l, 0, 0, 0), pipeline_mode=pl.Buffered(1))
    in_specs = [
        pl.BlockSpec(memory_space=pltpu.SMEM),
        pl.BlockSpec((WINDOW, ATTN_WIDTH), lambda i: (i, 0)),
        pl.BlockSpec((WINDOW, 2 * KV_WIDTH), lambda i: (i, 0)),
        pl.BlockSpec((WINDOW, 2 * KV_WIDTH), lambda i: (jnp.where(i % nb == 0, i, i - 1), 0)),
        cache,
        cache,
    ]
    out_specs = [pl.BlockSpec((WINDOW, ATTN_WIDTH), lambda i: (i, 0))]
    out_shape = [jax.ShapeDtypeStruct((T_ALL, ATTN_WIDTH), F32)]
    for w, layer, rows in casts:
        _, n_rows, n_cols = w.shape
        n_blocks = n_rows // rows
        assert n_blocks * rows == n_rows and _N_PROMPT_BLOCKS % n_blocks == 0
        steps = _N_PROMPT_BLOCKS // n_blocks
        blk = lambda i, steps=steps, n_blocks=n_blocks: jnp.minimum(i // steps, n_blocks - 1)
        in_specs.append(pl.BlockSpec((None, rows, n_cols), lambda i, blk=blk, layer=layer: (layer, blk(i), 0)))
        out_specs.append(pl.BlockSpec((rows, n_cols), lambda i, blk=blk: (blk(i), 0)))
        out_shape.append(jax.ShapeDtypeStruct((n_rows, n_cols), BF16))
    return pl.pallas_call(
        functools.partial(_attn_kernel, len(casts)),
        grid=(_N_PROMPT_BLOCKS + 1,),
        in_specs=in_specs,
        out_specs=out_specs,
        out_shape=out_shape,
        compiler_params=_cparams("arbitrary"),
        name="attention",
    )(sink, q_all, kv_all, kv_all, kc, vc, *[w for w, _, _ in casts])


def _expand_groups(m):
    m2 = m.reshape(SLAB, SSM_STATE)
    t = jnp.concatenate([m2] * GROUPS_PER_SLAB, axis=1)
    r = lax.broadcasted_iota(jnp.int32, (SLAB, SLAB_STATE), 0) // SSM_GROUP
    c = lax.broadcasted_iota(jnp.int32, (SLAB, SLAB_STATE), 1) // SSM_STATE
    return jnp.where(r == c, t, 0.0)


def _fold_groups(x):
    t = [x[:, i * SLAB:(i + 1) * SLAB] for i in range(2 * SLAB_STATE // SLAB)]
    return jnp.concatenate([t[0] + t[1] + t[2] + t[3], t[4] + t[5] + t[6] + t[7]], axis=1)


def _ssm_tables(lam_ref, logdt_ref, bre_ref, bim_ref, cre_ref, cim_ref, max_chunk):
    H = SLAB_STATE
    dt = jnp.exp(logdt_ref[...])
    lam = lam_ref[...]
    lr, li = lam[0:1], lam[1:2]
    n_tau = ((max_chunk + 1 + 7) // 8) * 8
    tau = lax.broadcasted_iota(jnp.int32, (n_tau, H), 0).astype(F32)
    mag = jnp.exp(tau * (lr * dt))
    ang = tau * (li * dt)
    pw_r, pw_i = mag * jnp.cos(ang), mag * jnp.sin(ang)
    abr, abi = pw_r[1:2], pw_i[1:2]
    nr = abr - 1.0
    den = lr * lr + li * li
    cr = (nr * lr + abi * li) / den
    ci = (abi * lr - nr * li) / den
    xb_r, xb_i = _expand_groups(bre_ref[...]), _expand_groups(bim_ref[...])
    xc_r, xc_i = _expand_groups(cre_ref[...]), _expand_groups(cim_ref[...])
    return pw_r, pw_i, cr, ci, xb_r, xb_i, xc_r, xc_i


def _ssm_operators(L, tables, tt_ref, bst_ref, cst_ref):
    pw_r, pw_i, cr, ci, xb_r, xb_i, xc_r, xc_i = tables
    ccm = _fold_groups(jnp.concatenate([xc_r, -xc_i], axis=1))
    same_group = (lax.broadcasted_iota(jnp.int32, (SLAB, SLAB), 0) // SSM_GROUP
                  == lax.broadcasted_iota(jnp.int32, (SLAB, SLAB), 1) // SSM_GROUP)

    k_lag = [None] * L
    for s in range(L):
        e = L - 1 - s
        w_r = cr * pw_r[e:e + 1] - ci * pw_i[e:e + 1]
        w_i = cr * pw_i[e:e + 1] + ci * pw_r[e:e + 1]
        slab = jnp.concatenate([xb_r * w_r - xb_i * w_i, xb_r * w_i + xb_i * w_r], axis=1)
        bst_ref[s * SLAB:(s + 1) * SLAB, :] = slab.astype(BF16)
        k_all = lax.dot_general(_fold_groups(slab), ccm, _TRANS_B, preferred_element_type=F32, precision=_HI)
        k_lag[e] = jnp.where(same_group, k_all, 0.0)
    for s in range(L):
        for t_lo in range(2):
            lag = L - 2 + t_lo - s
            blk = k_lag[lag] if lag >= 0 else jnp.zeros((SLAB, SLAB), F32)
            tt_ref[s * SLAB:(s + 1) * SLAB, t_lo * SLAB:(t_lo + 1) * SLAB] = blk.astype(BF16)
    for t in range(L):
        p_r, p_i = pw_r[t + 1:t + 2], pw_i[t + 1:t + 2]
        g_t = jnp.concatenate([xc_r * p_r - xc_i * p_i, -(xc_r * p_i + xc_i * p_r)], axis=1)
        cst_ref[:, t * SLAB:(t + 1) * SLAB] = g_t.T.astype(BF16)
    return pw_r[L:L + 1], pw_i[L:L + 1]


_SCAN_ROWS = 8


def _scan_chunk_states(s_r, s_i, a_r, a_i, n_per_seq):
    n, H = s_r.shape
    j8 = lax.broadcasted_iota(jnp.int32, (n, H), 0) % _SCAN_ROWS
    p_r, p_i = a_r, a_i
    d = 1
    while d < _SCAN_ROWS:
        keep = j8 >= d
        sh_r = jnp.where(keep, pltpu.roll(s_r, d, axis=0), 0.0)
        sh_i = jnp.where(keep, pltpu.roll(s_i, d, axis=0), 0.0)
        s_r, s_i = s_r + sh_r * p_r - sh_i * p_i, s_i + sh_r * p_i + sh_i * p_r
        p_r, p_i = p_r * p_r - p_i * p_i, 2.0 * p_r * p_i
        d *= 2
    rows_r, rows_i = [a_r], [a_i]
    for _ in range(_SCAN_ROWS - 1):
        q_r, q_i = rows_r[-1], rows_i[-1]
        rows_r.append(q_r * a_r - q_i * a_i)
        rows_i.append(q_r * a_i + q_i * a_r)
    t_r, t_i = jnp.concatenate(rows_r, axis=0), jnp.concatenate(rows_i, axis=0)
    out_r, out_i = [], []
    for g in range(n // _SCAN_ROWS):
        b_r = s_r[g * _SCAN_ROWS:(g + 1) * _SCAN_ROWS]
        b_i = s_i[g * _SCAN_ROWS:(g + 1) * _SCAN_ROWS]
        if (g * _SCAN_ROWS) % n_per_seq != 0:
            c_r, c_i = out_r[-1][_SCAN_ROWS - 1:], out_i[-1][_SCAN_ROWS - 1:]
            b_r, b_i = b_r + t_r * c_r - t_i * c_i, b_i + t_r * c_i + t_i * c_r
        out_r.append(b_r)
        out_i.append(b_i)
    return jnp.concatenate(out_r, axis=0), jnp.concatenate(out_i, axis=0)


def _ssm_rows(u_ref, y_ref, row0, n_rows, L, ops, n_per_seq=None, h0=None):
    tt_ref, bst_ref, cst_ref, (a_r, a_i) = ops
    H = SLAB_STATE
    nj = n_rows // L
    lhs = jnp.concatenate([u_ref[pl.ds(row0 + s, nj, stride=L), :].astype(BF16) for s in range(L)], axis=1)
    e = jnp.dot(lhs, bst_ref[...], preferred_element_type=F32)
    s_r, s_i = e[:, :H], e[:, H:]
    if h0 is not None:
        in_r, in_i = h0
        s_r = s_r + in_r * a_r - in_i * a_i
        s_i = s_i + in_r * a_i + in_i * a_r
    else:
        s_r, s_i = _scan_chunk_states(s_r, s_i, a_r, a_i, n_per_seq)
        jj = lax.broadcasted_iota(jnp.int32, (nj, H), 0) % n_per_seq
        in_r = jnp.where(jj >= 1, pltpu.roll(s_r, 1, axis=0), 0.0)
        in_i = jnp.where(jj >= 1, pltpu.roll(s_i, 1, axis=0), 0.0)
    y_st = jnp.dot(jnp.concatenate([in_r, in_i], axis=1).astype(BF16), cst_ref[...], preferred_element_type=F32)
    for tp in range(L // 2):
        k = 2 * SLAB * (tp + 1)
        y = jnp.dot(lhs[:, :k], tt_ref[2 * SLAB * (L // 2 - 1 - tp):, :], preferred_element_type=F32)
        y = y + y_st[:, 2 * SLAB * tp:2 * SLAB * (tp + 1)]
        y_ref[pl.ds(row0 + 2 * tp, nj, stride=L), :] = y[:, :SLAB]
        y_ref[pl.ds(row0 + 2 * tp + 1, nj, stride=L), :] = y[:, SLAB:]
    return s_r, s_i


def _ssm_kernel(u_ref, h0r_ref, h0i_ref, lam_ref, logdt_ref, bre_ref, bim_ref, cre_ref, cim_ref,
                y_ref, pr_ref, pi_ref, sr_ref, si_ref,
                tt_p, bst_p, cst_p, tt_s, bst_s, cst_s):
    tables = _ssm_tables(lam_ref, logdt_ref, bre_ref, bim_ref, cre_ref, cim_ref, max(CHUNK_P, CHUNK_S))
    al_p = _ssm_operators(CHUNK_P, tables, tt_p, bst_p, cst_p)
    al_s = _ssm_operators(CHUNK_S, tables, tt_s, bst_s, cst_s)
    n_per_seq = SEQ // CHUNK_P
    s_r, s_i = _ssm_rows(u_ref, y_ref, 0, T_PROMPT, CHUNK_P, (tt_p, bst_p, cst_p, al_p), n_per_seq=n_per_seq)
    for b in range(BATCH):
        last = (b + 1) * n_per_seq - 1
        pr_ref[b:b + 1, :] = s_r[last:last + 1]
        pi_ref[b:b + 1, :] = s_i[last:last + 1]
    n_r, n_i = _ssm_rows(u_ref, y_ref, T_PROMPT, T_SAMPLE, CHUNK_S, (tt_s, bst_s, cst_s, al_s),
                         h0=(h0r_ref[...], h0i_ref[...]))
    sr_ref[...] = n_r
    si_ref[...] = n_i


def _ssm_operator_scratch(L):
    H = SLAB_STATE
    return [pltpu.VMEM((L * SLAB, 2 * SLAB), BF16), pltpu.VMEM((L * SLAB, 2 * H), BF16),
            pltpu.VMEM((2 * H, L * SLAB), BF16)]


def _ssm(u_all, h0_r, h0_i, lam, logdt, bt_re, bt_im, c_re, c_im, l):
    H = SLAB_STATE
    slab = lambda rows: pl.BlockSpec((rows, SLAB), lambda q: (0, q))
    state = lambda rows: pl.BlockSpec((rows, H), lambda q: (0, q))
    state_in = pl.BlockSpec((None, DEC_BATCH, H), lambda q: (l, 0, q))
    q3 = lambda *blk: pl.BlockSpec((None, None) + blk, lambda q: (l, q, 0, 0))
    grp = pl.BlockSpec((None, GROUPS_PER_SLAB, SSM_GROUP, SSM_STATE), lambda q: (l, q, 0, 0))
    return pl.pallas_call(
        _ssm_kernel,
        grid=(N_SLABS,),
        in_specs=[slab(T_ALL), state_in, state_in, q3(2, H), q3(1, H), grp, grp, grp, grp],
        out_specs=[slab(T_ALL), state(BATCH), state(BATCH), state(DEC_BATCH), state(DEC_BATCH)],
        out_shape=[
            jax.ShapeDtypeStruct((T_ALL, SSM_WIDTH), F32),
            jax.ShapeDtypeStruct((BATCH, N_SLABS * H), F32),
            jax.ShapeDtypeStruct((BATCH, N_SLABS * H), F32),
            jax.ShapeDtypeStruct((DEC_BATCH, N_SLABS * H), F32),
            jax.ShapeDtypeStruct((DEC_BATCH, N_SLABS * H), F32),
        ],
        scratch_shapes=_ssm_operator_scratch(CHUNK_P) + _ssm_operator_scratch(CHUNK_S),
        compiler_params=_cparams("parallel"),
        name="ssm",
    )(u_all, h0_r, h0_i, lam, logdt, bt_re, bt_im, c_re, c_im)


MIX_TILE = 640


def _mix_kernel(y_ref, u_ref, d_ref, wg_ref, gs_ref, a_ref, ga_ref, wo_ref, x_ref, o_ref):
    a_n = _rms(a_ref[...], ga_ref[...]).astype(BF16)
    mixed = jnp.dot(a_n, wo_ref[:ATTN_WIDTH, :], preferred_element_type=F32)
    y = y_ref[...] + d_ref[...] * u_ref[...]
    z = 0.5 * y * (1.0 + lax.erf(y * (1.0 / math.sqrt(2.0))))
    gate = jnp.dot(z.astype(BF16), wg_ref[...], preferred_element_type=F32)
    s_n = _rms(z * jax.nn.sigmoid(gate), gs_ref[...]).astype(BF16)
    mixed = mixed + jnp.dot(s_n, wo_ref[ATTN_WIDTH:, :], preferred_element_type=F32)
    o_ref[...] = x_ref[...] + mixed


def _mix(y_all, u_all, d, w_glu_b, g_ssm, a_all, g_attn, w_out_b, x_all, l):
    row = lambda i: (i, 0)
    par = lambda i: (l, 0, 0)
    whole = lambda shape: pl.BlockSpec(shape, lambda i: (0, 0), pipeline_mode=pl.Buffered(1))
    return pl.pallas_call(
        _mix_kernel,
        grid=(T_ALL // MIX_TILE,),
        in_specs=[
            pl.BlockSpec((MIX_TILE, SSM_WIDTH), row),
            pl.BlockSpec((MIX_TILE, SSM_WIDTH), row),
            pl.BlockSpec((None, 1, SSM_WIDTH), par),
            whole((SSM_WIDTH, SSM_WIDTH)),
            pl.BlockSpec((None, 1, SSM_WIDTH), par),
            pl.BlockSpec((MIX_TILE, ATTN_WIDTH), row),
            pl.BlockSpec((None, 1, ATTN_WIDTH), par),
            whole((D_MODEL, D_MODEL)),
            pl.BlockSpec((MIX_TILE, D_MODEL), row),
        ],
        out_specs=pl.BlockSpec((MIX_TILE, D_MODEL), row),
        out_shape=jax.ShapeDtypeStruct((T_ALL, D_MODEL), F32),
        compiler_params=_cparams("parallel"),
        name="mix",
    )(y_all, u_all, d, w_glu_b, g_ssm, a_all, g_attn, w_out_b, x_all)


def _ffn_body(x_ref, g_ref, wg_ref, wu_ref, wd_ref, o_ref, h_ref):
    @pl.when(pl.program_id(1) == 0)
    def _():
        x = x_ref[...]
        h_ref[...] = _rms(x, g_ref[...]).astype(BF16)
        o_ref[...] = x

    h = h_ref[...]
    gate = jnp.dot(h, wg_ref[...], preferred_element_type=F32)
    up = jnp.dot(h, wu_ref[...], preferred_element_type=F32)
    act = (gate * jax.nn.sigmoid(gate) * up).astype(BF16)
    o_ref[...] += jnp.dot(act, wd_ref[...], preferred_element_type=F32)


def _ffn_final_kernel(x_ref, g_ref, wg_ref, wu_ref, wd_ref, gf_ref, yp_ref, ys_ref, h_ref):
    _ffn_body(x_ref, g_ref, wg_ref, wu_ref, wd_ref, yp_ref, h_ref)

    @pl.when(pl.program_id(1) == _N_FF_TILES - 1)
    def _():
        y = _rms(yp_ref[...], gf_ref[...])
        yp_ref[...] = y

        @pl.when(pl.program_id(0) == _N_ROW_TILES - 1)
        def _():
            ys_ref[...] = y[ROW_TILE - T_SAMPLE:]


def _ffn_cast_kernel(x_ref, g_ref, wg_ref, wu_ref, wd_ref, ng_ref, nu_ref, nd_ref,
                     o_ref, ngb_ref, nub_ref, ndb_ref, h_ref):
    _ffn_body(x_ref, g_ref, wg_ref, wu_ref, wd_ref, o_ref, h_ref)
    ngb_ref[...] = ng_ref[...].astype(BF16)
    nub_ref[...] = nu_ref[...].astype(BF16)
    ndb_ref[...] = nd_ref[...].astype(BF16)


_N_ROW_TILES = T_ALL // ROW_TILE
_N_FF_TILES = D_FF // FF_TILE
_CAST_ROWS = D_MODEL // _N_ROW_TILES


def _ffn(x_all, g, w_gate_b, w_up_b, w_down_b, l, w_next=None, g_final=None):
    in_specs = [
        pl.BlockSpec((ROW_TILE, D_MODEL), lambda i, f: (i, 0)),
        pl.BlockSpec((None, 1, D_MODEL), lambda i, f: (l, 0, 0)),
        pl.BlockSpec((D_MODEL, FF_TILE), lambda i, f: (0, f)),
        pl.BlockSpec((D_MODEL, FF_TILE), lambda i, f: (0, f)),
        pl.BlockSpec((FF_TILE, D_MODEL), lambda i, f: (f, 0)),
    ]
    out_specs = [pl.BlockSpec((ROW_TILE, D_MODEL), lambda i, f: (i, 0))]
    out_shape = [jax.ShapeDtypeStruct((T_ALL, D_MODEL), F32)]
    args = [x_all, g, w_gate_b, w_up_b, w_down_b]
    if w_next is not None:
        assert D_MODEL % _N_ROW_TILES == 0
        in_specs += [
            pl.BlockSpec((None, _CAST_ROWS, FF_TILE), lambda i, f: (l + 1, i, f)),
            pl.BlockSpec((None, _CAST_ROWS, FF_TILE), lambda i, f: (l + 1, i, f)),
            pl.BlockSpec((None, FF_TILE, _CAST_ROWS), lambda i, f: (l + 1, f, i)),
        ]
        out_specs += [
            pl.BlockSpec((_CAST_ROWS, FF_TILE), lambda i, f: (i, f)),
            pl.BlockSpec((_CAST_ROWS, FF_TILE), lambda i, f: (i, f)),
            pl.BlockSpec((FF_TILE, _CAST_ROWS), lambda i, f: (f, i)),
        ]
        out_shape += [jax.ShapeDtypeStruct((D_MODEL, D_FF), BF16), jax.ShapeDtypeStruct((D_MODEL, D_FF), BF16),
                      jax.ShapeDtypeStruct((D_FF, D_MODEL), BF16)]
        args += list(w_next)
    else:
        assert T_PROMPT > (_N_ROW_TILES - 1) * ROW_TILE and T_ALL == _N_ROW_TILES * ROW_TILE
        in_specs.append(pl.BlockSpec((1, D_MODEL), lambda i, f: (0, 0)))
        out_specs = [pl.BlockSpec((ROW_TILE, D_MODEL), lambda i, f: (i, 0)),
                     pl.BlockSpec((T_SAMPLE, D_MODEL), lambda i, f: (0, 0))]
        out_shape = [jax.ShapeDtypeStruct((T_PROMPT, D_MODEL), F32), jax.ShapeDtypeStruct((T_SAMPLE, D_MODEL), F32)]
        args.append(g_final)
    return pl.pallas_call(
        _ffn_final_kernel if w_next is None else _ffn_cast_kernel,
        grid=(_N_ROW_TILES, _N_FF_TILES),
        in_specs=in_specs,
        out_specs=out_specs,
        out_shape=out_shape,
        scratch_shapes=[pltpu.VMEM((ROW_TILE, D_MODEL), BF16)],
        compiler_params=_cparams("parallel" if w_next is not None else "arbitrary", "arbitrary"),
        name="ffn",
    )(*args)


def kernel(x_prompt, x_sample, cache_k, cache_v, state_ssm_re, state_ssm_im, norm_mix, w_in, attn_sink, ssm_a_re,
           ssm_a_im, ssm_log_dt, ssm_b_re, ssm_b_im, ssm_c_re, ssm_c_im, ssm_d, w_glu, norm_attn_out, norm_ssm_out,
           w_out, norm_ffn, w_gate, w_up, w_down, norm_final):
    G, P, H = N_SSM_GROUPS, SSM_STATE, SLAB_STATE
    w_in_b, w_out_b, w_glu_b = w_in[0].astype(BF16), w_out[0].astype(BF16), w_glu[0].astype(BF16)
    row3 = lambda p: p.reshape(DEPTH, 1, -1)
    g_mix, g_attn, g_ssm, g_ffn, d3 = row3(norm_mix), row3(norm_attn_out), row3(norm_ssm_out), row3(norm_ffn), row3(ssm_d)
    lam = jnp.stack([ssm_a_re.reshape(DEPTH, N_SLABS, H), ssm_a_im.reshape(DEPTH, N_SLABS, H)], axis=2)
    logdt = jnp.repeat(ssm_log_dt, P, axis=1).reshape(DEPTH, N_SLABS, 1, H)
    bt_re, bt_im = ssm_b_re.transpose(0, 1, 3, 2), ssm_b_im.transpose(0, 1, 3, 2)
    cache_k2 = cache_k.reshape(DEPTH, DEC_BATCH, WINDOW, KV_WIDTH)
    cache_v2 = cache_v.reshape(DEPTH, DEC_BATCH, WINDOW, KV_WIDTH)

    h0_r = state_ssm_re.reshape(DEPTH, DEC_BATCH, G * P)
    h0_i = state_ssm_im.reshape(DEPTH, DEC_BATCH, G * P)

    kv_p, kv_s, hr_p, hi_p, hr_s, hi_s = [], [], [], [], [], []
    for l in range(DEPTH):
        if l == 0:
            q_all, kv_all, u_all, x_all = _in_proj_first(x_prompt.reshape(T_PROMPT, D_MODEL),
                                                         x_sample.reshape(T_SAMPLE, D_MODEL), g_mix, w_in_b)
        else:
            q_all, kv_all, u_all = _in_proj(x_all, g_mix, w_in_b, l)

        casts = []
        if l == 0:
            casts += [(w_gate, 0, D_MODEL // 64), (w_up, 0, D_MODEL // 64), (w_down, 0, D_FF // 32)]
        if l + 1 < DEPTH:
            casts += [(w_in, l + 1, D_MODEL // 64), (w_out, l + 1, D_MODEL // 64), (w_glu, l + 1, SSM_WIDTH // 64)]
        a_all, *cast_out = _attention(attn_sink[l], q_all, kv_all, cache_k2, cache_v2, l, casts)
        if l == 0:
            ffn_w, cast_out = cast_out[:3], cast_out[3:]
        w_glu_cur, w_out_cur = w_glu_b, w_out_b
        if l + 1 < DEPTH:
            w_in_b, w_out_b, w_glu_b = cast_out

        y_all, sfin_r, sfin_i, snew_r, snew_i = _ssm(u_all, h0_r, h0_i, lam, logdt, bt_re, bt_im, ssm_c_re, ssm_c_im, l)
        x_all = _mix(y_all, u_all, d3, w_glu_cur, g_ssm, a_all, g_attn, w_out_cur, x_all, l)
        if l + 1 < DEPTH:
            x_all, *ffn_w = _ffn(x_all, g_ffn, *ffn_w, l, w_next=(w_gate, w_up, w_down))
        else:
            y_prompt, y_sample = _ffn(x_all, g_ffn, *ffn_w, l, g_final=norm_final.reshape(1, D_MODEL))

        kv_p += [kv_all[(b + 1) * SEQ - WINDOW:(b + 1) * SEQ] for b in range(BATCH)]
        kv_s.append(kv_all[T_PROMPT:])
        hr_p.append(sfin_r)
        hi_p.append(sfin_i)
        hr_s.append(snew_r)
        hi_s.append(snew_i)

    y_prompt = y_prompt.reshape(BATCH, SEQ, D_MODEL)
    y_sample = y_sample.reshape(DEC_BATCH, DEC_SEQ, D_MODEL)
    kv_p = jnp.stack(kv_p).reshape(DEPTH, BATCH, WINDOW, 2, N_KV_HEADS, HEAD_DIM)
    kv_s = jnp.stack(kv_s).reshape(DEPTH, DEC_BATCH, DEC_SEQ, 2, N_KV_HEADS, HEAD_DIM)
    k_s = jnp.concatenate([cache_k[:, :, DEC_SEQ:], kv_s[:, :, :, 0]], axis=2)
    v_s = jnp.concatenate([cache_v[:, :, DEC_SEQ:], kv_s[:, :, :, 1]], axis=2)
    states = lambda parts, batch: jnp.stack(parts).reshape(DEPTH, batch, G, P)
    return (y_prompt, y_sample, kv_p[:, :, :, 0], kv_p[:, :, :, 1], states(hr_p, BATCH), states(hi_p, BATCH),
            k_s, v_s, states(hr_s, DEC_BATCH), states(hi_s, DEC_BATCH))
```

```python
import functools
import math

import jax
import jax.numpy as jnp
from jax import lax
from jax.experimental import pallas as pl
from jax.experimental.pallas import tpu as pltpu

F32 = jnp.float32
BF16 = jnp.bfloat16

D_MODEL = 2048
BATCH = 2
SEQ = 4096
DEPTH = 4
DEC_BATCH = 32
DEC_SEQ = 4
ATTN_WIDTH = 1024
SSM_WIDTH = 1024
HEAD_DIM = 64
N_HEADS = 16
N_KV_HEADS = 2
GQA_GROUP = 8
KV_WIDTH = 128
WINDOW = 128
SSM_GROUP = 16
N_SSM_GROUPS = 64
SSM_STATE = 64
D_IN = ATTN_WIDTH + 2 * KV_WIDTH + SSM_WIDTH
D_FF = 5632
EPS = 1e-5

T_PROMPT = BATCH * SEQ
T_SAMPLE = DEC_BATCH * DEC_SEQ
T_ALL = T_PROMPT + T_SAMPLE
ROW_TILE = 1040
FF_TILE = 512
CHUNK_P = 8
CHUNK_S = DEC_SEQ
SLAB = 128
GROUPS_PER_SLAB = SLAB // SSM_GROUP
N_SLABS = SSM_WIDTH // SLAB
SLAB_STATE = GROUPS_PER_SLAB * SSM_STATE
MASK_NEG = -1e30
V7X_VMEM_BYTES = 64 * 1024 * 1024
VMEM_LIMIT = V7X_VMEM_BYTES - 4 * 1024 * 1024

_TRANS_B = (((1,), (1,)), ((), ()))
_HI = lax.Precision.HIGHEST


def _cparams(*sem):
    return pltpu.CompilerParams(dimension_semantics=sem, vmem_limit_bytes=VMEM_LIMIT)


def _rms(x, g):
    ms = jnp.mean(x * x, axis=-1, keepdims=True)
    return x * lax.rsqrt(ms + EPS) * g


def _in_proj_kernel(x_ref, g_ref, w_ref, q_ref, kv_ref, u_ref):
    h = _rms(x_ref[...], g_ref[...]).astype(BF16)
    z = jnp.dot(h, w_ref[...], preferred_element_type=F32)
    q_ref[...] = (z[:, :ATTN_WIDTH] * (HEAD_DIM ** -0.5)).astype(BF16)
    kv_ref[...] = z[:, ATTN_WIDTH:ATTN_WIDTH + 2 * KV_WIDTH]
    u_ref[...] = z[:, ATTN_WIDTH + 2 * KV_WIDTH:]


def _in_proj_out(tile):
    row = lambda i: (i, 0)
    specs = [pl.BlockSpec((tile, ATTN_WIDTH), row), pl.BlockSpec((tile, 2 * KV_WIDTH), row),
             pl.BlockSpec((tile, SSM_WIDTH), row)]
    shapes = [jax.ShapeDtypeStruct((T_ALL, ATTN_WIDTH), BF16), jax.ShapeDtypeStruct((T_ALL, 2 * KV_WIDTH), F32),
              jax.ShapeDtypeStruct((T_ALL, SSM_WIDTH), F32)]
    return specs, shapes


def _in_proj(x_all, g, w_in_b, l):
    out_specs, out_shapes = _in_proj_out(ROW_TILE)
    return pl.pallas_call(
        _in_proj_kernel,
        grid=(T_ALL // ROW_TILE,),
        in_specs=[
            pl.BlockSpec((ROW_TILE, D_MODEL), lambda i: (i, 0)),
            pl.BlockSpec((None, 1, D_MODEL), lambda i: (l, 0, 0)),
            pl.BlockSpec((D_MODEL, D_IN), lambda i: (0, 0), pipeline_mode=pl.Buffered(1)),
        ],
        out_specs=out_specs,
        out_shape=out_shapes,
        compiler_params=_cparams("parallel"),
        name="in_proj",
    )(x_all, g, w_in_b)


FIRST_TILE = 640
_LAST_TILE = T_ALL // FIRST_TILE - 1
_PROMPT_ROWS_LAST_TILE = T_PROMPT - _LAST_TILE * FIRST_TILE


def _in_proj_first_kernel(xp_ref, xs_ref, g_ref, w_ref, q_ref, kv_ref, u_ref, x_ref):
    i = pl.program_id(0)

    @pl.when(i < _LAST_TILE)
    def _():
        x_ref[...] = xp_ref[...]

    @pl.when(i == _LAST_TILE)
    def _():
        x_ref[:_PROMPT_ROWS_LAST_TILE, :] = xp_ref[:_PROMPT_ROWS_LAST_TILE, :]
        x_ref[_PROMPT_ROWS_LAST_TILE:, :] = xs_ref[...]

    _in_proj_kernel(x_ref, g_ref, w_ref, q_ref, kv_ref, u_ref)


def _in_proj_first(x_prompt, x_sample, g, w_in_b):
    assert FIRST_TILE - _PROMPT_ROWS_LAST_TILE == T_SAMPLE
    out_specs, out_shapes = _in_proj_out(FIRST_TILE)
    return pl.pallas_call(
        _in_proj_first_kernel,
        grid=(T_ALL // FIRST_TILE,),
        in_specs=[
            pl.BlockSpec((FIRST_TILE, D_MODEL), lambda i: (i, 0)),
            pl.BlockSpec((T_SAMPLE, D_MODEL), lambda i: (0, 0)),
            pl.BlockSpec((None, 1, D_MODEL), lambda i: (0, 0, 0)),
            pl.BlockSpec((D_MODEL, D_IN), lambda i: (0, 0), pipeline_mode=pl.Buffered(1)),
        ],
        out_specs=out_specs + [pl.BlockSpec((FIRST_TILE, D_MODEL), lambda i: (i, 0))],
        out_shape=out_shapes + [jax.ShapeDtypeStruct((T_ALL, D_MODEL), F32)],
        compiler_params=_cparams("parallel"),
        name="in_proj_first",
    )(x_prompt, x_sample, g, w_in_b)


def _softmax_pv(s_parts, bias_parts, v_parts, sinks, rows):
    p_parts = [[] for _ in s_parts]
    dens = []
    for g in range(GQA_GROUP):
        sl = slice(g * rows, (g + 1) * rows)
        sk = sinks[g]
        sg = [s[sl] + b for s, b in zip(s_parts, bias_parts)]
        m = sk
        for s in sg:
            m = jnp.maximum(jnp.max(s, axis=-1, keepdims=True), m)
        den = jnp.exp(sk - m)
        for j, s in enumerate(sg):
            p = jnp.exp(s - m)
            den = den + jnp.sum(p, axis=-1, keepdims=True)
            p_parts[j].append(p.astype(BF16))
        dens.append(den)
    o = None
    for j, v in enumerate(v_parts):
        pj = jnp.concatenate(p_parts[j], axis=0)
        oj = jnp.dot(pj, v, preferred_element_type=F32)
        o = oj if o is None else o + oj
    return [o[g * rows:(g + 1) * rows] / dens[g] for g in range(GQA_GROUP)]


def _head_cols(h):
    return slice(h * HEAD_DIM, (h + 1) * HEAD_DIM)


def _stack_heads(q, hk):
    return jnp.concatenate([q[:, _head_cols(hk * GQA_GROUP + g)] for g in range(GQA_GROUP)], axis=0)


def _attn_prompt_block(n, sinks, q, kvc, kvp):
    row = lax.broadcasted_iota(jnp.int32, (WINDOW, 2 * WINDOW), 0)
    col = lax.broadcasted_iota(jnp.int32, (WINDOW, 2 * WINDOW), 1)
    ok = (col > row) & (col <= row + WINDOW) & ((col >= WINDOW) | (n > 0))
    bias = jnp.where(ok, 0.0, MASK_NEG).astype(F32)
    outs = []
    for hk in range(N_KV_HEADS):
        ks, vs = _head_cols(hk), _head_cols(N_KV_HEADS + hk)
        k = jnp.concatenate([kvp[:, ks], kvc[:, ks]], axis=0).astype(BF16)
        v = jnp.concatenate([kvp[:, vs], kvc[:, vs]], axis=0).astype(BF16)
        for g in range(GQA_GROUP):
            h = hk * GQA_GROUP + g
            s = lax.dot_general(q[:, _head_cols(h)], k, _TRANS_B, preferred_element_type=F32) + bias
            m = jnp.maximum(jnp.max(s, axis=-1, keepdims=True), sinks[h])
            p = jnp.exp(s - m)
            den = jnp.sum(p, axis=-1, keepdims=True) + jnp.exp(sinks[h] - m)
            outs.append(jnp.dot(p.astype(BF16), v, preferred_element_type=F32) / den)
    return jnp.concatenate(outs, axis=1)


_SEQ_PER_CHUNK = 4
_ROWS_S = _SEQ_PER_CHUNK * DEC_SEQ


def _attn_sample_chunk(sinks, q, kv, kc, vc):
    r = lax.broadcasted_iota(jnp.int32, (_ROWS_S, _SEQ_PER_CHUNK * WINDOW), 0)
    c = lax.broadcasted_iota(jnp.int32, (_ROWS_S, _SEQ_PER_CHUNK * WINDOW), 1)
    ok_c = (c // WINDOW == r // DEC_SEQ) & (c % WINDOW > r % DEC_SEQ)
    bias_c = jnp.where(ok_c, 0.0, MASK_NEG).astype(F32)
    r2 = lax.broadcasted_iota(jnp.int32, (_ROWS_S, _ROWS_S), 0)
    c2 = lax.broadcasted_iota(jnp.int32, (_ROWS_S, _ROWS_S), 1)
    ok_n = (c2 // DEC_SEQ == r2 // DEC_SEQ) & (c2 % DEC_SEQ <= r2 % DEC_SEQ)
    bias_n = jnp.where(ok_n, 0.0, MASK_NEG).astype(F32)
    outs = []
    for hk in range(N_KV_HEADS):
        ks, vs = _head_cols(hk), _head_cols(N_KV_HEADS + hk)
        k_c, v_c = kc[:, ks].astype(BF16), vc[:, ks].astype(BF16)
        k_n, v_n = kv[:, ks].astype(BF16), kv[:, vs].astype(BF16)
        qs = _stack_heads(q, hk)
        s_c = lax.dot_general(qs, k_c, _TRANS_B, preferred_element_type=F32)
        s_n = lax.dot_general(qs, k_n, _TRANS_B, preferred_element_type=F32)
        outs += _softmax_pv([s_c, s_n], [bias_c, bias_n], [v_c, v_n],
                            sinks[hk * GQA_GROUP:(hk + 1) * GQA_GROUP], _ROWS_S)
    return jnp.concatenate(outs, axis=1)


_N_PROMPT_BLOCKS = T_PROMPT // WINDOW


def _attn_kernel(n_cast, sink_ref, q_ref, kvc_ref, kvp_ref, kc_ref, vc_ref, *rest):
    o_ref = rest[n_cast]
    for src_ref, dst_ref in zip(rest[:n_cast], rest[n_cast + 1:]):
        dst_ref[...] = src_ref[...].astype(BF16)
    i = pl.program_id(0)
    sinks = [sink_ref[h] for h in range(N_HEADS)]

    @pl.when(i < _N_PROMPT_BLOCKS)
    def _():
        o_ref[...] = _attn_prompt_block(i % (SEQ // WINDOW), sinks, q_ref[...], kvc_ref[...], kvp_ref[...])

    @pl.when(i == _N_PROMPT_BLOCKS)
    def _():
        def chunk(c, carry):
            rows = pl.ds(pl.multiple_of(c * _ROWS_S, _ROWS_S), _ROWS_S)
            seqs = pl.ds(c * _SEQ_PER_CHUNK, _SEQ_PER_CHUNK)
            kc = kc_ref[seqs].reshape(_SEQ_PER_CHUNK * WINDOW, KV_WIDTH)
            vc = vc_ref[seqs].reshape(_SEQ_PER_CHUNK * WINDOW, KV_WIDTH)
            o_ref[rows, :] = _attn_sample_chunk(sinks, q_ref[rows, :], kvc_ref[rows, :], kc, vc)
            return carry
        lax.fori_loop(0, DEC_BATCH // _SEQ_PER_CHUNK, chunk, 0)


def _attention(sink, q_all, kv_all, kc, vc, l, casts=()):
    nb = SEQ // WINDOW
    cache = pl.BlockSpec((None, DEC_BATCH, WINDOW, KV_WIDTH), lambda i: (l, 0, 0, 0), pipeline_mode=pl.Buffered(1))
    in_specs = [
        pl.BlockSpec(memory_space=pltpu.SMEM),
        pl.BlockSpec((WINDOW, ATTN_WIDTH), lambda i: (i, 0)),
        pl.BlockSpec((WINDOW, 2 * KV_WIDTH), lambda i: (i, 0)),
        pl.BlockSpec((WINDOW, 2 * KV_WIDTH), lambda i: (jnp.where(i % nb == 0, i, i - 1), 0)),
        cache,
        cache,
    ]
    out_specs = [pl.BlockSpec((WINDOW, ATTN_WIDTH), lambda i: (i, 0))]
    out_shape = [jax.ShapeDtypeStruct((T_ALL, ATTN_WIDTH), F32)]
    for w, layer, rows in casts:
        _, n_rows, n_cols = w.shape
        n_blocks = n_rows // rows
        assert n_blocks * rows == n_rows and _N_PROMPT_BLOCKS % n_blocks == 0
        steps = _N_PROMPT_BLOCKS // n_blocks
        blk = lambda i, steps=steps, n_blocks=n_blocks: jnp.minimum(i // steps, n_blocks - 1)
        in_specs.append(pl.BlockSpec((None, rows, n_cols), lambda i, blk=blk, layer=layer: (layer, blk(i), 0)))
        out_specs.append(pl.BlockSpec((rows, n_cols), lambda i, blk=blk: (blk(i), 0)))
        out_shape.append(jax.ShapeDtypeStruct((n_rows, n_cols), BF16))
    return pl.pallas_call(
        functools.partial(_attn_kernel, len(casts)),
        grid=(_N_PROMPT_BLOCKS + 1,),
        in_specs=in_specs,
        out_specs=out_specs,
        out_shape=out_shape,
        compiler_params=_cparams("arbitrary"),
        name="attention",
    )(sink, q_all, kv_all, kv_all, kc, vc, *[w for w, _, _ in casts])


def _expand_groups(m):
    m2 = m.reshape(SLAB, SSM_STATE)
    t = jnp.concatenate([m2] * GROUPS_PER_SLAB, axis=1)
    r = lax.broadcasted_iota(jnp.int32, (SLAB, SLAB_STATE), 0) // SSM_GROUP
    c = lax.broadcasted_iota(jnp.int32, (SLAB, SLAB_STATE), 1) // SSM_STATE
    return jnp.where(r == c, t, 0.0)


def _fold_groups(x):
    t = [x[:, i * SLAB:(i + 1) * SLAB] for i in range(2 * SLAB_STATE // SLAB)]
    return jnp.concatenate([t[0] + t[1] + t[2] + t[3], t[4] + t[5] + t[6] + t[7]], axis=1)


def _ssm_tables(lam_ref, logdt_ref, bre_ref, bim_ref, cre_ref, cim_ref, max_chunk):
    H = SLAB_STATE
    dt = jnp.exp(logdt_ref[...])
    lam = lam_ref[...]
    lr, li = lam[0:1], lam[1:2]
    n_tau = ((max_chunk + 1 + 7) // 8) * 8
    tau = lax.broadcasted_iota(jnp.int32, (n_tau, H), 0).astype(F32)
    mag = jnp.exp(tau * (lr * dt))
    ang = tau * (li * dt)
    pw_r, pw_i = mag * jnp.cos(ang), mag * jnp.sin(ang)
    abr, abi = pw_r[1:2], pw_i[1:2]
    nr = abr - 1.0
    den = lr * lr + li * li
    cr = (nr * lr + abi * li) / den
    ci = (abi * lr - nr * li) / den
    xb_r, xb_i = _expand_groups(bre_ref[...]), _expand_groups(bim_ref[...])
    xc_r, xc_i = _expand_groups(cre_ref[...]), _expand_groups(cim_ref[...])
    return pw_r, pw_i, cr, ci, xb_r, xb_i, xc_r, xc_i


def _ssm_operators(L, tables, tt_ref, bst_ref, cst_ref):
    pw_r, pw_i, cr, ci, xb_r, xb_i, xc_r, xc_i = tables
    ccm = _fold_groups(jnp.concatenate([xc_r, -xc_i], axis=1))
    same_group = (lax.broadcasted_iota(jnp.int32, (SLAB, SLAB), 0) // SSM_GROUP
                  == lax.broadcasted_iota(jnp.int32, (SLAB, SLAB), 1) // SSM_GROUP)

    k_lag = [None] * L
    for s in range(L):
        e = L - 1 - s
        w_r = cr * pw_r[e:e + 1] - ci * pw_i[e:e + 1]
        w_i = cr * pw_i[e:e + 1] + ci * pw_r[e:e + 1]
        slab = jnp.concatenate([xb_r * w_r - xb_i * w_i, xb_r * w_i + xb_i * w_r], axis=1)
        bst_ref[s * SLAB:(s + 1) * SLAB, :] = slab.astype(BF16)
        k_all = lax.dot_general(_fold_groups(slab), ccm, _TRANS_B, preferred_element_type=F32, precision=_HI)
        k_lag[e] = jnp.where(same_group, k_all, 0.0)
    for s in range(L):
        for t_lo in range(2):
            lag = L - 2 + t_lo - s
            blk = k_lag[lag] if lag >= 0 else jnp.zeros((SLAB, SLAB), F32)
            tt_ref[s * SLAB:(s + 1) * SLAB, t_lo * SLAB:(t_lo + 1) * SLAB] = blk.astype(BF16)
    for t in range(L):
        p_r, p_i = pw_r[t + 1:t + 2], pw_i[t + 1:t + 2]
        g_t = jnp.concatenate([xc_r * p_r - xc_i * p_i, -(xc_r * p_i + xc_i * p_r)], axis=1)
        cst_ref[:, t * SLAB:(t + 1) * SLAB] = g_t.T.astype(BF16)
    return pw_r[L:L + 1], pw_i[L:L + 1]


_SCAN_ROWS = 8


def _scan_chunk_states(s_r, s_i, a_r, a_i, n_per_seq):
    n, H = s_r.shape
    j8 = lax.broadcasted_iota(jnp.int32, (n, H), 0) % _SCAN_ROWS
    p_r, p_i = a_r, a_i
    d = 1
    while d < _SCAN_ROWS:
        keep = j8 >= d
        sh_r = jnp.where(keep, pltpu.roll(s_r, d, axis=0), 0.0)
        sh_i = jnp.where(keep, pltpu.roll(s_i, d, axis=0), 0.0)
        s_r, s_i = s_r + sh_r * p_r - sh_i * p_i, s_i + sh_r * p_i + sh_i * p_r
        p_r, p_i = p_r * p_r - p_i * p_i, 2.0 * p_r * p_i
        d *= 2
    rows_r, rows_i = [a_r], [a_i]
    for _ in range(_SCAN_ROWS - 1):
        q_r, q_i = rows_r[-1], rows_i[-1]
        rows_r.append(q_r * a_r - q_i * a_i)
        rows_i.append(q_r * a_i + q_i * a_r)
    t_r, t_i = jnp.concatenate(rows_r, axis=0), jnp.concatenate(rows_i, axis=0)
    out_r, out_i = [], []
    for g in range(n // _SCAN_ROWS):
        b_r = s_r[g * _SCAN_ROWS:(g + 1) * _SCAN_ROWS]
        b_i = s_i[g * _SCAN_ROWS:(g + 1) * _SCAN_ROWS]
        if (g * _SCAN_ROWS) % n_per_seq != 0:
            c_r, c_i = out_r[-1][_SCAN_ROWS - 1:], out_i[-1][_SCAN_ROWS - 1:]
            b_r, b_i = b_r + t_r * c_r - t_i * c_i, b_i + t_r * c_i + t_i * c_r
        out_r.append(b_r)
        out_i.append(b_i)
    return jnp.concatenate(out_r, axis=0), jnp.concatenate(out_i, axis=0)


def _ssm_rows(u_ref, y_ref, row0, n_rows, L, ops, n_per_seq=None, h0=None):
    tt_ref, bst_ref, cst_ref, (a_r, a_i) = ops
    H = SLAB_STATE
    nj = n_rows // L
    lhs = jnp.concatenate([u_ref[pl.ds(row0 + s, nj, stride=L), :].astype(BF16) for s in range(L)], axis=1)
    e = jnp.dot(lhs, bst_ref[...], preferred_element_type=F32)
    s_r, s_i = e[:, :H], e[:, H:]
    if h0 is not None:
        in_r, in_i = h0
        s_r = s_r + in_r * a_r - in_i * a_i
        s_i = s_i + in_r * a_i + in_i * a_r
    else:
        s_r, s_i = _scan_chunk_states(s_r, s_i, a_r, a_i, n_per_seq)
        jj = lax.broadcasted_iota(jnp.int32, (nj, H), 0) % n_per_seq
        in_r = jnp.where(jj >= 1, pltpu.roll(s_r, 1, axis=0), 0.0)
        in_i = jnp.where(jj >= 1, pltpu.roll(s_i, 1, axis=0), 0.0)
    y_st = jnp.dot(jnp.concatenate([in_r, in_i], axis=1).astype(BF16), cst_ref[...], preferred_element_type=F32)
    for tp in range(L // 2):
        k = 2 * SLAB * (tp + 1)
        y = jnp.dot(lhs[:, :k], tt_ref[2 * SLAB * (L // 2 - 1 - tp):, :], preferred_element_type=F32)
        y = y + y_st[:, 2 * SLAB * tp:2 * SLAB * (tp + 1)]
        y_ref[pl.ds(row0 + 2 * tp, nj, stride=L), :] = y[:, :SLAB]
        y_ref[pl.ds(row0 + 2 * tp + 1, nj, stride=L), :] = y[:, SLAB:]
    return s_r, s_i


def _ssm_kernel(u_ref, h0r_ref, h0i_ref, lam_ref, logdt_ref, bre_ref, bim_ref, cre_ref, cim_ref,
                y_ref, pr_ref, pi_ref, sr_ref, si_ref,
                tt_p, bst_p, cst_p, tt_s, bst_s, cst_s):
    tables = _ssm_tables(lam_ref, logdt_ref, bre_ref, bim_ref, cre_ref, cim_ref, max(CHUNK_P, CHUNK_S))
    al_p = _ssm_operators(CHUNK_P, tables, tt_p, bst_p, cst_p)
    al_s = _ssm_operators(CHUNK_S, tables, tt_s, bst_s, cst_s)
    n_per_seq = SEQ // CHUNK_P
    s_r, s_i = _ssm_rows(u_ref, y_ref, 0, T_PROMPT, CHUNK_P, (tt_p, bst_p, cst_p, al_p), n_per_seq=n_per_seq)
    for b in range(BATCH):
        last = (b + 1) * n_per_seq - 1
        pr_ref[b:b + 1, :] = s_r[last:last + 1]
        pi_ref[b:b + 1, :] = s_i[last:last + 1]
    n_r, n_i = _ssm_rows(u_ref, y_ref, T_PROMPT, T_SAMPLE, CHUNK_S, (tt_s, bst_s, cst_s, al_s),
                         h0=(h0r_ref[...], h0i_ref[...]))
    sr_ref[...] = n_r
    si_ref[...] = n_i


def _ssm_operator_scratch(L):
    H = SLAB_STATE
    return [pltpu.VMEM((L * SLAB, 2 * SLAB), BF16), pltpu.VMEM((L * SLAB, 2 * H), BF16),
            pltpu.VMEM((2 * H, L * SLAB), BF16)]


def _ssm(u_all, h0_r, h0_i, lam, logdt, bt_re, bt_im, c_re, c_im, l):
    H = SLAB_STATE
    slab = lambda rows: pl.BlockSpec((rows, SLAB), lambda q: (0, q))
    state = lambda rows: pl.BlockSpec((rows, H), lambda q: (0, q))
    state_in = pl.BlockSpec((None, DEC_BATCH, H), lambda q: (l, 0, q))
    q3 = lambda *blk: pl.BlockSpec((None, None) + blk, lambda q: (l, q, 0, 0))
    grp = pl.BlockSpec((None, GROUPS_PER_SLAB, SSM_GROUP, SSM_STATE), lambda q: (l, q, 0, 0))
    return pl.pallas_call(
        _ssm_kernel,
        grid=(N_SLABS,),
        in_specs=[slab(T_ALL), state_in, state_in, q3(2, H), q3(1, H), grp, grp, grp, grp],
        out_specs=[slab(T_ALL), state(BATCH), state(BATCH), state(DEC_BATCH), state(DEC_BATCH)],
        out_shape=[
            jax.ShapeDtypeStruct((T_ALL, SSM_WIDTH), F32),
            jax.ShapeDtypeStruct((BATCH, N_SLABS * H), F32),
            jax.ShapeDtypeStruct((BATCH, N_SLABS * H), F32),
            jax.ShapeDtypeStruct((DEC_BATCH, N_SLABS * H), F32),
            jax.ShapeDtypeStruct((DEC_BATCH, N_SLABS * H), F32),
        ],
        scratch_shapes=_ssm_operator_scratch(CHUNK_P) + _ssm_operator_scratch(CHUNK_S),
        compiler_params=_cparams("parallel"),
        name="ssm",
    )(u_all, h0_r, h0_i, lam, logdt, bt_re, bt_im, c_re, c_im)


MIX_TILE = 640


_X_RING = 3


def _mix_kernel(y_ref, u_ref, d_ref, wg_ref, gs_ref, a_ref, ga_ref, wo_ref, x_hbm, o_ref, x_ring, x_sem):
    s = pl.program_id(0)
    n_steps = pl.num_programs(0)

    def x_copy(step):
        rows = pl.ds(pl.multiple_of(step * MIX_TILE, MIX_TILE), MIX_TILE)
        slot = step % _X_RING
        return pltpu.make_async_copy(x_hbm.at[rows, :], x_ring.at[slot], x_sem.at[slot])

    @pl.when(s == 0)
    def _():
        x_copy(0).start()
        x_copy(1).start()

    @pl.when(s + (_X_RING - 1) < n_steps)
    def _():
        x_copy(s + (_X_RING - 1)).start()

    x_copy(s).wait()
    a_n = _rms(a_ref[...], ga_ref[...]).astype(BF16)
    mixed = jnp.dot(a_n, wo_ref[:ATTN_WIDTH, :], preferred_element_type=F32)
    y = y_ref[...] + d_ref[...] * u_ref[...]
    z = 0.5 * y * (1.0 + lax.erf(y * (1.0 / math.sqrt(2.0))))
    gate = jnp.dot(z.astype(BF16), wg_ref[...], preferred_element_type=F32)
    s_n = _rms(z * jax.nn.sigmoid(gate), gs_ref[...]).astype(BF16)
    mixed = mixed + jnp.dot(s_n, wo_ref[ATTN_WIDTH:, :], preferred_element_type=F32)
    o_ref[...] = x_ring[s % _X_RING] + mixed


def _mix(y_all, u_all, d, w_glu_b, g_ssm, a_all, g_attn, w_out_b, x_all, l):
    row = lambda i: (i, 0)
    par = lambda i: (l, 0, 0)
    whole = lambda shape: pl.BlockSpec(shape, lambda i: (0, 0), pipeline_mode=pl.Buffered(1))
    return pl.pallas_call(
        _mix_kernel,
        grid=(T_ALL // MIX_TILE,),
        in_specs=[
            pl.BlockSpec((MIX_TILE, SSM_WIDTH), row),
            pl.BlockSpec((MIX_TILE, SSM_WIDTH), row),
            pl.BlockSpec((None, 1, SSM_WIDTH), par),
            whole((SSM_WIDTH, SSM_WIDTH)),
            pl.BlockSpec((None, 1, SSM_WIDTH), par),
            pl.BlockSpec((MIX_TILE, ATTN_WIDTH), row),
            pl.BlockSpec((None, 1, ATTN_WIDTH), par),
            whole((D_MODEL, D_MODEL)),
            pl.BlockSpec(memory_space=pl.ANY),
        ],
        out_specs=pl.BlockSpec((MIX_TILE, D_MODEL), row),
        out_shape=jax.ShapeDtypeStruct((T_ALL, D_MODEL), F32),
        scratch_shapes=[pltpu.VMEM((_X_RING, MIX_TILE, D_MODEL), F32), pltpu.SemaphoreType.DMA((_X_RING,))],
        compiler_params=_cparams("arbitrary"),
        name="mix",
    )(y_all, u_all, d, w_glu_b, g_ssm, a_all, g_attn, w_out_b, x_all)


def _ffn_body(x_ref, g_ref, wg_ref, wu_ref, wd_ref, o_ref, h_ref):
    @pl.when(pl.program_id(1) == 0)
    def _():
        x = x_ref[...]
        h_ref[...] = _rms(x, g_ref[...]).astype(BF16)
        o_ref[...] = x

    h = h_ref[...]
    gate = jnp.dot(h, wg_ref[...], preferred_element_type=F32)
    up = jnp.dot(h, wu_ref[...], preferred_element_type=F32)
    act = (gate * jax.nn.sigmoid(gate) * up).astype(BF16)
    o_ref[...] += jnp.dot(act, wd_ref[...], preferred_element_type=F32)


def _ffn_final_kernel(x_ref, g_ref, wg_ref, wu_ref, wd_ref, gf_ref, yp_ref, ys_ref, h_ref):
    _ffn_body(x_ref, g_ref, wg_ref, wu_ref, wd_ref, yp_ref, h_ref)

    @pl.when(pl.program_id(1) == _N_FF_TILES - 1)
    def _():
        y = _rms(yp_ref[...], gf_ref[...])
        yp_ref[...] = y

        @pl.when(pl.program_id(0) == _N_ROW_TILES - 1)
        def _():
            ys_ref[...] = y[ROW_TILE - T_SAMPLE:]


def _ffn_cast_kernel(x_ref, g_ref, wg_ref, wu_ref, wd_ref, ng_ref, nu_ref, nd_ref,
                     o_ref, ngb_ref, nub_ref, ndb_ref, h_ref):
    _ffn_body(x_ref, g_ref, wg_ref, wu_ref, wd_ref, o_ref, h_ref)
    ngb_ref[...] = ng_ref[...].astype(BF16)
    nub_ref[...] = nu_ref[...].astype(BF16)
    ndb_ref[...] = nd_ref[...].astype(BF16)


_N_ROW_TILES = T_ALL // ROW_TILE
_N_FF_TILES = D_FF // FF_TILE
_CAST_ROWS = D_MODEL // _N_ROW_TILES


def _ffn(x_all, g, w_gate_b, w_up_b, w_down_b, l, w_next=None, g_final=None):
    in_specs = [
        pl.BlockSpec((ROW_TILE, D_MODEL), lambda i, f: (i, 0)),
        pl.BlockSpec((None, 1, D_MODEL), lambda i, f: (l, 0, 0)),
        pl.BlockSpec((D_MODEL, FF_TILE), lambda i, f: (0, f)),
        pl.BlockSpec((D_MODEL, FF_TILE), lambda i, f: (0, f)),
        pl.BlockSpec((FF_TILE, D_MODEL), lambda i, f: (f, 0)),
    ]
    out_specs = [pl.BlockSpec((ROW_TILE, D_MODEL), lambda i, f: (i, 0))]
    out_shape = [jax.ShapeDtypeStruct((T_ALL, D_MODEL), F32)]
    args = [x_all, g, w_gate_b, w_up_b, w_down_b]
    if w_next is not None:
        assert D_MODEL % _N_ROW_TILES == 0
        in_specs += [
            pl.BlockSpec((None, _CAST_ROWS, FF_TILE), lambda i, f: (l + 1, i, f)),
            pl.BlockSpec((None, _CAST_ROWS, FF_TILE), lambda i, f: (l + 1, i, f)),
            pl.BlockSpec((None, FF_TILE, _CAST_ROWS), lambda i, f: (l + 1, f, i)),
        ]
        out_specs += [
            pl.BlockSpec((_CAST_ROWS, FF_TILE), lambda i, f: (i, f)),
            pl.BlockSpec((_CAST_ROWS, FF_TILE), lambda i, f: (i, f)),
            pl.BlockSpec((FF_TILE, _CAST_ROWS), lambda i, f: (f, i)),
        ]
        out_shape += [jax.ShapeDtypeStruct((D_MODEL, D_FF), BF16), jax.ShapeDtypeStruct((D_MODEL, D_FF), BF16),
                      jax.ShapeDtypeStruct((D_FF, D_MODEL), BF16)]
        args += list(w_next)
    else:
        assert T_PROMPT > (_N_ROW_TILES - 1) * ROW_TILE and T_ALL == _N_ROW_TILES * ROW_TILE
        in_specs.append(pl.BlockSpec((1, D_MODEL), lambda i, f: (0, 0)))
        out_specs = [pl.BlockSpec((ROW_TILE, D_MODEL), lambda i, f: (i, 0)),
                     pl.BlockSpec((T_SAMPLE, D_MODEL), lambda i, f: (0, 0))]
        out_shape = [jax.ShapeDtypeStruct((T_PROMPT, D_MODEL), F32), jax.ShapeDtypeStruct((T_SAMPLE, D_MODEL), F32)]
        args.append(g_final)
    return pl.pallas_call(
        _ffn_final_kernel if w_next is None else _ffn_cast_kernel,
        grid=(_N_ROW_TILES, _N_FF_TILES),
        in_specs=in_specs,
        out_specs=out_specs,
        out_shape=out_shape,
        scratch_shapes=[pltpu.VMEM((ROW_TILE, D_MODEL), BF16)],
        compiler_params=_cparams("parallel" if w_next is not None else "arbitrary", "arbitrary"),
        name="ffn",
    )(*args)


def kernel(x_prompt, x_sample, cache_k, cache_v, state_ssm_re, state_ssm_im, norm_mix, w_in, attn_sink, ssm_a_re,
           ssm_a_im, ssm_log_dt, ssm_b_re, ssm_b_im, ssm_c_re, ssm_c_im, ssm_d, w_glu, norm_attn_out, norm_ssm_out,
           w_out, norm_ffn, w_gate, w_up, w_down, norm_final):
    G, P, H = N_SSM_GROUPS, SSM_STATE, SLAB_STATE
    w_in_b, w_out_b, w_glu_b = w_in[0].astype(BF16), w_out[0].astype(BF16), w_glu[0].astype(BF16)
    row3 = lambda p: p.reshape(DEPTH, 1, -1)
    g_mix, g_attn, g_ssm, g_ffn, d3 = row3(norm_mix), row3(norm_attn_out), row3(norm_ssm_out), row3(norm_ffn), row3(ssm_d)
    lam = jnp.stack([ssm_a_re.reshape(DEPTH, N_SLABS, H), ssm_a_im.reshape(DEPTH, N_SLABS, H)], axis=2)
    logdt = jnp.repeat(ssm_log_dt, P, axis=1).reshape(DEPTH, N_SLABS, 1, H)
    bt_re, bt_im = ssm_b_re.transpose(0, 1, 3, 2), ssm_b_im.transpose(0, 1, 3, 2)
    cache_k2 = cache_k.reshape(DEPTH, DEC_BATCH, WINDOW, KV_WIDTH)
    cache_v2 = cache_v.reshape(DEPTH, DEC_BATCH, WINDOW, KV_WIDTH)

    h0_r = state_ssm_re.reshape(DEPTH, DEC_BATCH, G * P)
    h0_i = state_ssm_im.reshape(DEPTH, DEC_BATCH, G * P)

    kv_p, kv_s, hr_p, hi_p, hr_s, hi_s = [], [], [], [], [], []
    for l in range(DEPTH):
        if l == 0:
            q_all, kv_all, u_all, x_all = _in_proj_first(x_prompt.reshape(T_PROMPT, D_MODEL),
                                                         x_sample.reshape(T_SAMPLE, D_MODEL), g_mix, w_in_b)
        else:
            q_all, kv_all, u_all = _in_proj(x_all, g_mix, w_in_b, l)

        casts = []
        if l == 0:
            casts += [(w_gate, 0, D_MODEL // 64), (w_up, 0, D_MODEL // 64), (w_down, 0, D_FF // 32)]
        if l + 1 < DEPTH:
            casts += [(w_in, l + 1, D_MODEL // 64), (w_out, l + 1, D_MODEL // 64), (w_glu, l + 1, SSM_WIDTH // 64)]
        a_all, *cast_out = _attention(attn_sink[l], q_all, kv_all, cache_k2, cache_v2, l, casts)
        if l == 0:
            ffn_w, cast_out = cast_out[:3], cast_out[3:]
        w_glu_cur, w_out_cur = w_glu_b, w_out_b
        if l + 1 < DEPTH:
            w_in_b, w_out_b, w_glu_b = cast_out

        y_all, sfin_r, sfin_i, snew_r, snew_i = _ssm(u_all, h0_r, h0_i, lam, logdt, bt_re, bt_im, ssm_c_re, ssm_c_im, l)
        x_all = _mix(y_all, u_all, d3, w_glu_cur, g_ssm, a_all, g_attn, w_out_cur, x_all, l)
        if l + 1 < DEPTH:
            x_all, *ffn_w = _ffn(x_all, g_ffn, *ffn_w, l, w_next=(w_gate, w_up, w_down))
        else:
            y_prompt, y_sample = _ffn(x_all, g_ffn, *ffn_w, l, g_final=norm_final.reshape(1, D_MODEL))

        kv_p += [kv_all[(b + 1) * SEQ - WINDOW:(b + 1) * SEQ] for b in range(BATCH)]
        kv_s.append(kv_all[T_PROMPT:])
        hr_p.append(sfin_r)
        hi_p.append(sfin_i)
        hr_s.append(snew_r)
        hi_s.append(snew_i)

    y_prompt = y_prompt.reshape(BATCH, SEQ, D_MODEL)
    y_sample = y_sample.reshape(DEC_BATCH, DEC_SEQ, D_MODEL)
    kv_p = jnp.stack(kv_p).reshape(DEPTH, BATCH, WINDOW, 2, N_KV_HEADS, HEAD_DIM)
    kv_s = jnp.stack(kv_s).reshape(DEPTH, DEC_BATCH, DEC_SEQ, 2, N_KV_HEADS, HEAD_DIM)
    k_s = jnp.concatenate([cache_k[:, :, DEC_SEQ:], kv_s[:, :, :, 0]], axis=2)
    v_s = jnp.concatenate([cache_v[:, :, DEC_SEQ:], kv_s[:, :, :, 1]], axis=2)
    states = lambda parts, batch: jnp.stack(parts).reshape(DEPTH, batch, G, P)
    return (y_prompt, y_sample, kv_p[:, :, :, 0], kv_p[:, :, :, 1], states(hr_p, BATCH), states(hi_p, BATCH),
            k_s, v_s, states(hr_s, DEC_BATCH), states(hi_s, DEC_BATCH))
```
